```python
import math
import jax, jax.numpy as jnp
from jax import lax
import numpy as np

D_MODEL = 4096
BATCH = 1
SEQ = 8192
DEPTH = 2
DEC_BATCH = 8
DEC_SEQ = 2048
PAST_LEN = 128

MIX = D_MODEL
ATT_QK = 64
ATT_V = 2 * ATT_QK
ATT_W = 3 * MIX // 8
ATT_HEADS = ATT_W // ATT_V
Q_BLOCK = 128
HY_W = MIX // 4
HY_ORDER = 2
HY_GROUPS = 8
HY_EMB = 33
HY_FFN = 64
HY_TARGET = 1e-2
HY_FAST = 0.3
HY_SLOW = 1.5
ML_W = MIX - ATT_W - HY_W
ML_V = 256
ML_HEADS = ML_W // ML_V
ML_QK = ML_V // 2
ML_CHUNK = 64
N_EXPERTS = 64
TOP_K = 8
N_GROUPS = 8
TOPK_GROUPS = 4
D_EXPERT = D_MODEL // 4
D_SHARED = D_MODEL // 4
ROUTED_SCALE = 2.5
MOE_BLOCK = 256
N_MOD = 6
EPS = 1e-6
N_IN = 3 * ATT_W + 3 * HY_W + 2 * ML_HEADS * ML_QK + 2 * ML_W + 4 * ML_HEADS

kernel_name = 'hybrid_bidir_encoder_two_batches'


def rms_norm(x, g):
    xf = x.astype(jnp.float32)
    y = xf * lax.rsqrt(jnp.mean(xf * xf, axis=-1, keepdims=True) + EPS)
    return (y * g.astype(jnp.float32)).astype(x.dtype)


def diff_attention(q, k, v, qk_g, lam_p, out_g, lam_init):
    B, S, _ = q.shape
    q = rms_norm(q.reshape(B, S, ATT_HEADS, 2, ATT_QK), qk_g[:2])
    k = rms_norm(k.reshape(B, S, ATT_HEADS, 2, ATT_QK), qk_g[2:])
    v = v.reshape(B, S, ATT_HEADS, ATT_V)
    lp = lam_p.astype(jnp.float32)
    lam = jnp.exp(jnp.sum(lp[0] * lp[1])) - jnp.exp(jnp.sum(lp[2] * lp[3])) + lam_init
    slopes = jnp.asarray(2.0 ** (-8.0 * np.arange(1, ATT_HEADS + 1) / ATT_HEADS), jnp.float32)
    scale = ATT_QK ** -0.5
    nq = S // Q_BLOCK
    kpos = jnp.arange(S)
    q_blocks = jnp.moveaxis(q.reshape(B, nq, Q_BLOCK, ATT_HEADS, 2, ATT_QK), 1, 0)
    q_pos = kpos.reshape(nq, Q_BLOCK)

    def block(args):
        qb, qpos = args
        dist = jnp.abs(qpos[:, None] - kpos[None, :]).astype(jnp.float32)
        bias = -slopes[:, None, None, None] * dist
        s = jnp.einsum('bqhmd,bkhmd->bhmqk', qb, k, preferred_element_type=jnp.float32) * scale + bias
        p = jax.nn.softmax(s, axis=-1)
        w = p[:, :, 0] - lam * p[:, :, 1]
        return jnp.einsum('bhqk,bkhv->bqhv', w.astype(v.dtype), v)

    out = lax.map(block, (q_blocks, q_pos))
    out = jnp.moveaxis(out, 0, 1).reshape(B, S, ATT_HEADS, ATT_V)
    out = rms_norm(out, out_g) * (1.0 - lam_init)
    return out.reshape(B, S, ATT_W)


def hyena_filter_fft(L, w1, b1, freq, w2, b2, w3, b3):
    f32 = jnp.float32
    t = jnp.linspace(0.0, 1.0, L, dtype=f32)[:, None]
    bands = (HY_EMB - 1) // 2
    fr = jnp.linspace(1e-4, bands - 1, bands, dtype=f32)
    ang = (2.0 * math.pi / L) * jnp.arange(L, dtype=f32)[:, None] * fr[None, :]
    z = jnp.concatenate([t, jnp.cos(ang), -jnp.sin(ang)], axis=-1)
    fq = freq.astype(f32)
    h = jnp.sin(fq[0] * (z @ w1.astype(f32) + b1.astype(f32)))
    h = jnp.sin(fq[1] * (h @ w2.astype(f32) + b2.astype(f32)))
    h = (h @ w3.astype(f32) + b3.astype(f32)).reshape(L, HY_ORDER, 2, HY_W)
    deltas = jnp.linspace(math.log(HY_TARGET) / HY_SLOW, math.log(HY_TARGET) / HY_FAST, HY_W, dtype=f32)
    h = h * jnp.exp(-t * jnp.abs(deltas))[:, None, None, :]
    hf, hb = h[:, :, 0], h[:, :, 1]
    circ = jnp.concatenate([hf[:1] + hb[:1], hf[1:], jnp.zeros_like(hf[:1]), hb[1:][::-1]], axis=0)
    return jnp.fft.rfft(circ, axis=0)


def hyena(u, conv_w, conv_b, filt, hy_bias, out_g):
    B, L, _ = u.shape
    dtype = u.dtype
    up = jnp.pad(u, ((0, 0), (1, 1), (0, 0)))
    u = up[:, :-2] * conv_w[0] + up[:, 1:-1] * conv_w[1] + up[:, 2:] * conv_w[2] + conv_b
    v, x1, x2 = jnp.split(u.astype(jnp.float32), 3, axis=-1)
    d = hy_bias.astype(jnp.float32)

    def long_conv(z, o):
        zf = jnp.fft.rfft(z, n=2 * L, axis=1)
        return jnp.fft.irfft(zf * filt[None, :, o], n=2 * L, axis=1)[:, :L] + d[o] * z

    z = x1 * long_conv(v, 0)
    z = x2 * long_conv(z, 1)
    z = rms_norm(z.reshape(B, L, HY_GROUPS, HY_W // HY_GROUPS), out_g.reshape(HY_GROUPS, HY_W // HY_GROUPS))
    return z.reshape(B, L, HY_W).astype(dtype)


def mlstm_scan(q, k, v, log_i, log_f):
    B, H, S, dk = q.shape
    dv = v.shape[-1]
    nc = S // ML_CHUNK

    def to_chunks(a):
        return jnp.moveaxis(a.reshape((B, H, nc, ML_CHUNK) + a.shape[3:]), 2, 0)

    mask = jnp.tril(jnp.ones((ML_CHUNK, ML_CHUNK), dtype=bool))

    def step(carry, xs):
        C, n, m = carry
        qc, kc, vc, ic, fc = xs
        a = jnp.cumsum(fc, axis=-1)
        g = a[..., -1]
        dmat = jnp.where(mask, a[..., :, None] - a[..., None, :] + ic[..., None, :], -jnp.inf)
        m_inter = a + m[..., None]
        mj = jnp.maximum(jnp.max(dmat, axis=-1), m_inter)
        p = jnp.exp(dmat - mj[..., None])
        wts = p * jnp.einsum('bhjd,bhsd->bhjs', qc, kc)
        inter = jnp.exp(m_inter - mj)
        num = jnp.einsum('bhjs,bhsv->bhjv', wts, vc) + inter[..., None] * jnp.einsum('bhjd,bhdv->bhjv', qc, C)
        den = jnp.sum(wts, axis=-1) + inter * jnp.einsum('bhjd,bhd->bhj', qc, n)
        h = num / jnp.maximum(jnp.abs(den), jnp.exp(-mj))[..., None]
        ws = g[..., None] - a + ic
        m_new = jnp.maximum(g + m, jnp.max(ws, axis=-1))
        dec = jnp.exp(g + m - m_new)
        e = jnp.exp(ws - m_new[..., None])
        C = dec[..., None, None] * C + jnp.einsum('bhs,bhsd,bhsv->bhdv', e, kc, vc)
        n = dec[..., None] * n + jnp.einsum('bhs,bhsd->bhd', e, kc)
        return (C, n, m_new), h

    init = (jnp.zeros((B, H, dk, dv), jnp.float32), jnp.zeros((B, H, dk), jnp.float32),
            jnp.zeros((B, H), jnp.float32))
    xs = (to_chunks(q), to_chunks(k), to_chunks(v), to_chunks(log_i), to_chunks(log_f))
    _, h = lax.scan(step, init, xs)
    return jnp.moveaxis(h, 0, 2).reshape(B, H, S, dv)


def mlstm(q, k, v, o, gates, gate_b, out_g):
    B, S, _ = q.shape
    f32 = jnp.float32

    def heads(a, d):
        return jnp.transpose(a.astype(f32).reshape(B, S, ML_HEADS, d), (0, 2, 1, 3))

    q = heads(q, ML_QK) * (ML_QK ** -0.5)
    k = heads(k, ML_QK)
    v = heads(v, ML_V)
    g = jnp.transpose(gates.astype(f32).reshape(B, S, 4, ML_HEADS) + gate_b.astype(f32), (2, 0, 3, 1))
    i_f, f_f, i_b, f_b = g[0], g[1], g[2], g[3]
    h_f = mlstm_scan(q, k, v, i_f, jax.nn.log_sigmoid(f_f))

    def flip(a):
        return jnp.flip(a, axis=2)

    h_b = flip(mlstm_scan(flip(q), flip(k), flip(v), flip(i_b), flip(jax.nn.log_sigmoid(f_b))))
    h = jnp.transpose(h_f + h_b, (0, 2, 1, 3))
    h = rms_norm(h, out_g.reshape(ML_HEADS, ML_V)).reshape(B, S, ML_W)
    return (jax.nn.sigmoid(o.astype(f32)) * h).astype(o.dtype)


def mixer(h, w_in, w_out, qk_g, lam_p, att_g, conv_w, conv_b, filt, hy_bias, hy_g, ml_gate_b, ml_g, lam_init):
    proj = h @ w_in
    sizes = [ATT_W, ATT_W, ATT_W, 3 * HY_W, ML_HEADS * ML_QK, ML_HEADS * ML_QK, ML_W, ML_W, 4 * ML_HEADS]
    cuts = np.cumsum(sizes)[:-1].tolist()
    aq, ak, av, hy_in, mq, mk, mv, mo, mg = jnp.split(proj, cuts, axis=-1)
    ya = diff_attention(aq, ak, av, qk_g, lam_p, att_g, lam_init)
    yb = hyena(hy_in, conv_w, conv_b, filt, hy_bias, hy_g)
    yc = mlstm(mq, mk, mv, mo, mg, ml_gate_b, ml_g)
    return jnp.concatenate([ya, yb, yc], axis=-1) @ w_out


def moe(x, router_w, router_b, wg, wu, wd, swg, swu, swd):
    B, S, D = x.shape
    T = B * S
    xt = x.reshape(T, D)
    s = jax.nn.sigmoid(xt.astype(jnp.float32) @ router_w.astype(jnp.float32))
    sb = s + router_b.astype(jnp.float32)
    gscore = jnp.sum(lax.top_k(sb.reshape(T, N_GROUPS, N_EXPERTS // N_GROUPS), 2)[0], axis=-1)
    _, gidx = lax.top_k(gscore, TOPK_GROUPS)
    gmask = jnp.sum(jax.nn.one_hot(gidx, N_GROUPS, dtype=jnp.float32), axis=1) > 0
    emask = jnp.repeat(gmask, N_EXPERTS // N_GROUPS, axis=1)
    _, idx = lax.top_k(jnp.where(emask, sb, -jnp.inf), TOP_K)
    gate = jnp.take_along_axis(s, idx, axis=1)
    gate = gate / jnp.sum(gate, axis=-1, keepdims=True) * ROUTED_SCALE
    N = T * TOP_K
    e_flat = idx.reshape(N)
    tok = jnp.arange(N) // TOP_K
    w_flat = gate.reshape(N)
    order = jnp.argsort(e_flat)
    se = e_flat[order]
    counts = jnp.bincount(e_flat, length=N_EXPERTS)
    starts = jnp.cumsum(counts) - counts
    padded = (counts + MOE_BLOCK - 1) // MOE_BLOCK * MOE_BLOCK
    pends = jnp.cumsum(padded)
    pstarts = pends - padded
    dest = pstarts[se] + jnp.arange(N) - starts[se]
    nb = (N + N_EXPERTS * (MOE_BLOCK - 1) + MOE_BLOCK - 1) // MOE_BLOCK
    NP = nb * MOE_BLOCK
    row_tok = jnp.full((NP,), T, jnp.int32).at[dest].set(tok[order].astype(jnp.int32))
    row_w = jnp.zeros((NP,), jnp.float32).at[dest].set(w_flat[order])
    block_e = jnp.minimum(jnp.searchsorted(pends, jnp.arange(nb) * MOE_BLOCK, side='right'), N_EXPERTS - 1)
    x_pad = jnp.concatenate([xt, jnp.zeros((1, D), xt.dtype)], axis=0)

    def run_block(args):
        rows, e = args
        xb = x_pad[rows]
        hb = jax.nn.silu(xb @ wg[e]) * (xb @ wu[e])
        return hb @ wd[e]

    y_rows = lax.map(run_block, (row_tok.reshape(nb, MOE_BLOCK), block_e))
    y = jax.ops.segment_sum(y_rows.reshape(NP, D) * row_w[:, None].astype(x.dtype), row_tok,
                            num_segments=T + 1)[:T]
    shared = (jax.nn.silu(xt @ swg) * (xt @ swu)) @ swd
    return (y + shared).reshape(B, S, D)


def encoder_layer(x, c, l, lam_init, ada_w, ada_b, norm1_g, norm2_g, w_in, w_out, qk_norm_g, diff_lambda,
                  attn_out_g, hy_conv_w, hy_conv_b, hy_f_w1, hy_f_b1, hy_f_freq, hy_f_w2, hy_f_b2, hy_f_w3,
                  hy_f_b3, hy_bias, hy_out_g, ml_gate_b, ml_out_g, router_w, router_b, exp_w_gate, exp_w_up,
                  exp_w_down, sh_w_gate, sh_w_up, sh_w_down):
    B, L, D = x.shape
    mod = jax.nn.silu(c.astype(jnp.float32)) @ ada_w[l].astype(jnp.float32) + ada_b[l].astype(jnp.float32)
    mod = mod.reshape(B, N_MOD, 1, D).astype(x.dtype)
    filt = hyena_filter_fft(L, hy_f_w1[l], hy_f_b1[l], hy_f_freq[l], hy_f_w2[l], hy_f_b2[l], hy_f_w3[l], hy_f_b3[l])
    h = rms_norm(x, norm1_g[l]) * (1 + mod[:, 1]) + mod[:, 0]
    x = x + mod[:, 2] * mixer(h, w_in[l], w_out[l], qk_norm_g[l], diff_lambda[l], attn_out_g[l], hy_conv_w[l],
                              hy_conv_b[l], filt, hy_bias[l], hy_out_g[l], ml_gate_b[l], ml_out_g[l], lam_init)
    h = rms_norm(x, norm2_g[l]) * (1 + mod[:, 4]) + mod[:, 3]
    x = x + mod[:, 5] * moe(h, router_w[l], router_b[l], exp_w_gate[l], exp_w_up[l], exp_w_down[l],
                            sh_w_gate[l], sh_w_up[l], sh_w_down[l])
    return x


def setup_inputs(seed: int = 0) -> dict:
    key = jax.random.key(seed)
    keys = jax.random.split(key, 34)
    f32 = jnp.float32
    D = D_MODEL

    def nrm(i, shape, std):
        return std * jax.random.normal(keys[i], shape, f32)

    def gain(i, shape):
        return 1.0 + nrm(i, shape, 0.02)

    f_base = jnp.linspace(3.0, 6.0, ML_HEADS, dtype=f32)
    zero_h = jnp.zeros((ML_HEADS,), f32)
    gate_base = jnp.stack([zero_h, f_base, zero_h, f_base])
    return {
        'x_prompt': nrm(0, (BATCH, SEQ, D), 1.0),
        'x_sample': nrm(1, (DEC_BATCH, DEC_SEQ, D), 1.0),
        'c_prompt': nrm(2, (BATCH, D), 1.0),
        'c_sample': nrm(3, (DEC_BATCH, D), 1.0),
        'ada_w': nrm(4, (DEPTH, D, N_MOD * D), 0.3 * D ** -0.5),
        'ada_b': nrm(5, (DEPTH, N_MOD * D), 0.02),
        'norm1_g': gain(6, (DEPTH, D)),
        'norm2_g': gain(7, (DEPTH, D)),
        'w_in': nrm(8, (DEPTH, D, N_IN), D ** -0.5),
        'w_out': nrm(9, (DEPTH, MIX, D), MIX ** -0.5),
        'qk_norm_g': gain(10, (DEPTH, 4, ATT_QK)),
        'diff_lambda': nrm(11, (DEPTH, 4, ATT_QK), 0.1),
        'attn_out_g': gain(12, (DEPTH, ATT_V)),
        'hy_conv_w': nrm(13, (DEPTH, 3, 3 * HY_W), 3 ** -0.5),
        'hy_conv_b': nrm(14, (DEPTH, 3 * HY_W), 0.02),
        'hy_f_w1': nrm(15, (DEPTH, HY_EMB, HY_FFN), HY_EMB ** -0.5),
        'hy_f_b1': nrm(16, (DEPTH, HY_FFN), 0.02),
        'hy_f_freq': 1.0 + nrm(17, (DEPTH, 2, HY_FFN), 0.1),
        'hy_f_w2': nrm(18, (DEPTH, HY_FFN, HY_FFN), HY_FFN ** -0.5),
        'hy_f_b2': nrm(19, (DEPTH, HY_FFN), 0.02),
        'hy_f_w3': nrm(20, (DEPTH, HY_FFN, HY_ORDER * 2 * HY_W), HY_FFN ** -0.5),
        'hy_f_b3': nrm(21, (DEPTH, HY_ORDER * 2 * HY_W), 0.02),
        'hy_bias': nrm(22, (DEPTH, HY_ORDER, HY_W), 1.0),
        'hy_out_g': gain(23, (DEPTH, HY_W)),
        'ml_gate_b': gate_base + nrm(24, (DEPTH, 4, ML_HEADS), 0.1),
        'ml_out_g': gain(25, (DEPTH, ML_W)),
        'router_w': nrm(26, (DEPTH, D, N_EXPERTS), D ** -0.5),
        'router_b': nrm(27, (DEPTH, N_EXPERTS), 0.01),
        'exp_w_gate': nrm(28, (DEPTH, N_EXPERTS, D, D_EXPERT), D ** -0.5),
        'exp_w_up': nrm(29, (DEPTH, N_EXPERTS, D, D_EXPERT), D ** -0.5),
        'exp_w_down': nrm(30, (DEPTH, N_EXPERTS, D_EXPERT, D), D_EXPERT ** -0.5),
        'sh_w_gate': nrm(31, (DEPTH, D, D_SHARED), D ** -0.5),
        'sh_w_up': nrm(32, (DEPTH, D, D_SHARED), D ** -0.5),
        'sh_w_down': nrm(33, (DEPTH, D_SHARED, D), D_SHARED ** -0.5),
    }


def reference(x_prompt, x_sample, c_prompt, c_sample, ada_w, ada_b, norm1_g, norm2_g, w_in, w_out, qk_norm_g,
              diff_lambda, attn_out_g, hy_conv_w, hy_conv_b, hy_f_w1, hy_f_b1, hy_f_freq, hy_f_w2, hy_f_b2,
              hy_f_w3, hy_f_b3, hy_bias, hy_out_g, ml_gate_b, ml_out_g, router_w, router_b, exp_w_gate,
              exp_w_up, exp_w_down, sh_w_gate, sh_w_up, sh_w_down):
    y_prompt = x_prompt
    y_sample = x_sample
    for l in range(DEPTH):
        lam_init = 0.8 - 0.6 * math.exp(-0.3 * l)
        y_prompt = encoder_layer(y_prompt, c_prompt, l, lam_init, ada_w, ada_b, norm1_g, norm2_g, w_in, w_out,
                                 qk_norm_g, diff_lambda, attn_out_g, hy_conv_w, hy_conv_b, hy_f_w1, hy_f_b1,
                                 hy_f_freq, hy_f_w2, hy_f_b2, hy_f_w3, hy_f_b3, hy_bias, hy_out_g, ml_gate_b,
                                 ml_out_g, router_w, router_b, exp_w_gate, exp_w_up, exp_w_down, sh_w_gate,
                                 sh_w_up, sh_w_down)
        y_sample = encoder_layer(y_sample, c_sample, l, lam_init, ada_w, ada_b, norm1_g, norm2_g, w_in, w_out,
                                 qk_norm_g, diff_lambda, attn_out_g, hy_conv_w, hy_conv_b, hy_f_w1, hy_f_b1,
                                 hy_f_freq, hy_f_w2, hy_f_b2, hy_f_w3, hy_f_b3, hy_bias, hy_out_g, ml_gate_b,
                                 ml_out_g, router_w, router_b, exp_w_gate, exp_w_up, exp_w_down, sh_w_gate,
                                 sh_w_up, sh_w_down)
    return (y_prompt, y_sample)
```

```python
import math
import jax, jax.numpy as jnp
from jax import lax
import numpy as np
from jax.experimental import pallas as pl
from jax.experimental.pallas import tpu as pltpu

D_MODEL = 4096
DEPTH = 2
MIX = D_MODEL
ATT_QK = 64
ATT_V = 2 * ATT_QK
ATT_W = 3 * MIX // 8
ATT_HEADS = ATT_W // ATT_V
Q_BLOCK = 128
HY_W = MIX // 4
HY_ORDER = 2
HY_GROUPS = 8
HY_EMB = 33
HY_FFN = 64
HY_TARGET = 1e-2
HY_FAST = 0.3
HY_SLOW = 1.5
ML_W = MIX - ATT_W - HY_W
ML_V = 256
ML_HEADS = ML_W // ML_V
ML_QK = ML_V // 2
ML_CHUNK = 64
N_EXPERTS = 64
TOP_K = 8
N_GROUPS = 8
TOPK_GROUPS = 4
ROUTED_SCALE = 2.5
MOE_BLOCK = 256
N_MOD = 6
EPS = 1e-6
N_IN = 3 * ATT_W + 3 * HY_W + 2 * ML_HEADS * ML_QK + 2 * ML_W + 4 * ML_HEADS


def _mm_body(a_ref, b_ref, o_ref):
    o_ref[...] = jnp.dot(a_ref[...], b_ref[...], preferred_element_type=jnp.float32).astype(o_ref.dtype)


def pallas_matmul(a, b, tm=512, tn=512):
    M, K = a.shape
    _, N = b.shape
    return pl.pallas_call(
        _mm_body,
        grid=(M // tm, pl.cdiv(N, tn)),
        in_specs=[pl.BlockSpec((tm, K), lambda i, j: (i, 0)), pl.BlockSpec((K, tn), lambda i, j: (0, j))],
        out_specs=pl.BlockSpec((tm, tn), lambda i, j: (i, j)),
        out_shape=jax.ShapeDtypeStruct((M, N), jnp.float32),
        compiler_params=pltpu.CompilerParams(dimension_semantics=("parallel", "parallel"),
                                             vmem_limit_bytes=56 * 1024 * 1024),
    )(a, b)


def rms_norm(x, g):
    xf = x.astype(jnp.float32)
    y = xf * lax.rsqrt(jnp.mean(xf * xf, axis=-1, keepdims=True) + EPS)
    return (y * g.astype(jnp.float32)).astype(x.dtype)


def diff_attention(q, k, v, qk_g, lam_p, out_g, lam_init):
    B, S, _ = q.shape
    q = rms_norm(q.reshape(B, S, ATT_HEADS, 2, ATT_QK), qk_g[:2])
    k = rms_norm(k.reshape(B, S, ATT_HEADS, 2, ATT_QK), qk_g[2:])
    v = v.reshape(B, S, ATT_HEADS, ATT_V)
    lp = lam_p.astype(jnp.float32)
    lam = jnp.exp(jnp.sum(lp[0] * lp[1])) - jnp.exp(jnp.sum(lp[2] * lp[3])) + lam_init
    slopes = jnp.asarray(2.0 ** (-8.0 * np.arange(1, ATT_HEADS + 1) / ATT_HEADS), jnp.float32)
    scale = ATT_QK ** -0.5
    nq = S // Q_BLOCK
    kpos = jnp.arange(S)
    q_blocks = jnp.moveaxis(q.reshape(B, nq, Q_BLOCK, ATT_HEADS, 2, ATT_QK), 1, 0)
    q_pos = kpos.reshape(nq, Q_BLOCK)

    def block(args):
        qb, qpos = args
        dist = jnp.abs(qpos[:, None] - kpos[None, :]).astype(jnp.float32)
        bias = -slopes[:, None, None, None] * dist
        s = jnp.einsum('bqhmd,bkhmd->bhmqk', qb, k, preferred_element_type=jnp.float32) * scale + bias
        p = jax.nn.softmax(s, axis=-1)
        w = p[:, :, 0] - lam * p[:, :, 1]
        return jnp.einsum('bhqk,bkhv->bqhv', w.astype(v.dtype), v)

    out = lax.map(block, (q_blocks, q_pos))
    out = jnp.moveaxis(out, 0, 1).reshape(B, S, ATT_HEADS, ATT_V)
    out = rms_norm(out, out_g) * (1.0 - lam_init)
    return out.reshape(B, S, ATT_W)


def hyena_filter_fft(L, w1, b1, freq, w2, b2, w3, b3):
    f32 = jnp.float32
    t = jnp.linspace(0.0, 1.0, L, dtype=f32)[:, None]
    bands = (HY_EMB - 1) // 2
    fr = jnp.linspace(1e-4, bands - 1, bands, dtype=f32)
    ang = (2.0 * math.pi / L) * jnp.arange(L, dtype=f32)[:, None] * fr[None, :]
    z = jnp.concatenate([t, jnp.cos(ang), -jnp.sin(ang)], axis=-1)
    fq = freq.astype(f32)
    h = jnp.sin(fq[0] * (z @ w1.astype(f32) + b1.astype(f32)))
    h = jnp.sin(fq[1] * (h @ w2.astype(f32) + b2.astype(f32)))
    h = (h @ w3.astype(f32) + b3.astype(f32)).reshape(L, HY_ORDER, 2, HY_W)
    deltas = jnp.linspace(math.log(HY_TARGET) / HY_SLOW, math.log(HY_TARGET) / HY_FAST, HY_W, dtype=f32)
    h = h * jnp.exp(-t * jnp.abs(deltas))[:, None, None, :]
    hf, hb = h[:, :, 0], h[:, :, 1]
    circ = jnp.concatenate([hf[:1] + hb[:1], hf[1:], jnp.zeros_like(hf[:1]), hb[1:][::-1]], axis=0)
    return jnp.fft.rfft(circ, axis=0)


def hyena(u, conv_w, conv_b, filt, hy_bias, out_g):
    B, L, _ = u.shape
    dtype = u.dtype
    up = jnp.pad(u, ((0, 0), (1, 1), (0, 0)))
    u = up[:, :-2] * conv_w[0] + up[:, 1:-1] * conv_w[1] + up[:, 2:] * conv_w[2] + conv_b
    v, x1, x2 = jnp.split(u.astype(jnp.float32), 3, axis=-1)
    d = hy_bias.astype(jnp.float32)

    def long_conv(z, o):
        zf = jnp.fft.rfft(z, n=2 * L, axis=1)
        return jnp.fft.irfft(zf * filt[None, :, o], n=2 * L, axis=1)[:, :L] + d[o] * z

    z = x1 * long_conv(v, 0)
    z = x2 * long_conv(z, 1)
    z = rms_norm(z.reshape(B, L, HY_GROUPS, HY_W // HY_GROUPS), out_g.reshape(HY_GROUPS, HY_W // HY_GROUPS))
    return z.reshape(B, L, HY_W).astype(dtype)


def mlstm_scan(q, k, v, log_i, log_f):
    B, H, S, dk = q.shape
    dv = v.shape[-1]
    nc = S // ML_CHUNK

    def to_chunks(a):
        return jnp.moveaxis(a.reshape((B, H, nc, ML_CHUNK) + a.shape[3:]), 2, 0)

    mask = jnp.tril(jnp.ones((ML_CHUNK, ML_CHUNK), dtype=bool))

    def step(carry, xs):
        C, n, m = carry
        qc, kc, vc, ic, fc = xs
        a = jnp.cumsum(fc, axis=-1)
        g = a[..., -1]
        dmat = jnp.where(mask, a[..., :, None] - a[..., None, :] + ic[..., None, :], -jnp.inf)
        m_inter = a + m[..., None]
        mj = jnp.maximum(jnp.max(dmat, axis=-1), m_inter)
        p = jnp.exp(dmat - mj[..., None])
        wts = p * jnp.einsum('bhjd,bhsd->bhjs', qc, kc)
        inter = jnp.exp(m_inter - mj)
        num = jnp.einsum('bhjs,bhsv->bhjv', wts, vc) + inter[..., None] * jnp.einsum('bhjd,bhdv->bhjv', qc, C)
        den = jnp.sum(wts, axis=-1) + inter * jnp.einsum('bhjd,bhd->bhj', qc, n)
        h = num / jnp.maximum(jnp.abs(den), jnp.exp(-mj))[..., None]
        ws = g[..., None] - a + ic
        m_new = jnp.maximum(g + m, jnp.max(ws, axis=-1))
        dec = jnp.exp(g + m - m_new)
        e = jnp.exp(ws - m_new[..., None])
        C = dec[..., None, None] * C + jnp.einsum('bhs,bhsd,bhsv->bhdv', e, kc, vc)
        n = dec[..., None] * n + jnp.einsum('bhs,bhsd->bhd', e, kc)
        return (C, n, m_new), h

    init = (jnp.zeros((B, H, dk, dv), jnp.float32), jnp.zeros((B, H, dk), jnp.float32),
            jnp.zeros((B, H), jnp.float32))
    xs = (to_chunks(q), to_chunks(k), to_chunks(v), to_chunks(log_i), to_chunks(log_f))
    _, h = lax.scan(step, init, xs)
    return jnp.moveaxis(h, 0, 2).reshape(B, H, S, dv)


def mlstm(q, k, v, o, gates, gate_b, out_g):
    B, S, _ = q.shape
    f32 = jnp.float32

    def heads(a, d):
        return jnp.transpose(a.astype(f32).reshape(B, S, ML_HEADS, d), (0, 2, 1, 3))

    q = heads(q, ML_QK) * (ML_QK ** -0.5)
    k = heads(k, ML_QK)
    v = heads(v, ML_V)
    g = jnp.transpose(gates.astype(f32).reshape(B, S, 4, ML_HEADS) + gate_b.astype(f32), (2, 0, 3, 1))
    i_f, f_f, i_b, f_b = g[0], g[1], g[2], g[3]
    h_f = mlstm_scan(q, k, v, i_f, jax.nn.log_sigmoid(f_f))

    def flip(a):
        return jnp.flip(a, axis=2)

    h_b = flip(mlstm_scan(flip(q), flip(k), flip(v), flip(i_b), flip(jax.nn.log_sigmoid(f_b))))
    h = jnp.transpose(h_f + h_b, (0, 2, 1, 3))
    h = rms_norm(h, out_g.reshape(ML_HEADS, ML_V)).reshape(B, S, ML_W)
    return (jax.nn.sigmoid(o.astype(f32)) * h).astype(o.dtype)


def mixer(h, w_in, w_out, qk_g, lam_p, att_g, conv_w, conv_b, filt, hy_bias, hy_g, ml_gate_b, ml_g, lam_init):
    Bm, Lm, _ = h.shape
    proj = pallas_matmul(h.reshape(Bm * Lm, D_MODEL).astype(jnp.bfloat16),
                         w_in.astype(jnp.bfloat16)).reshape(Bm, Lm, N_IN)
    sizes = [ATT_W, ATT_W, ATT_W, 3 * HY_W, ML_HEADS * ML_QK, ML_HEADS * ML_QK, ML_W, ML_W, 4 * ML_HEADS]
    cuts = np.cumsum(sizes)[:-1].tolist()
    aq, ak, av, hy_in, mq, mk, mv, mo, mg = jnp.split(proj, cuts, axis=-1)
    ya = diff_attention(aq, ak, av, qk_g, lam_p, att_g, lam_init)
    yb = hyena(hy_in, conv_w, conv_b, filt, hy_bias, hy_g)
    yc = mlstm(mq, mk, mv, mo, mg, ml_gate_b, ml_g)
    ycat = jnp.concatenate([ya, yb, yc], axis=-1).reshape(Bm * Lm, MIX).astype(jnp.bfloat16)
    return pallas_matmul(ycat, w_out.astype(jnp.bfloat16)).reshape(Bm, Lm, D_MODEL)


def moe(x, router_w, router_b, wg, wu, wd, swg, swu, swd):
    B, S, D = x.shape
    T = B * S
    xt = x.reshape(T, D)
    s = jax.nn.sigmoid(xt.astype(jnp.float32) @ router_w.astype(jnp.float32))
    sb = s + router_b.astype(jnp.float32)
    gscore = jnp.sum(lax.top_k(sb.reshape(T, N_GROUPS, N_EXPERTS // N_GROUPS), 2)[0], axis=-1)
    _, gidx = lax.top_k(gscore, TOPK_GROUPS)
    gmask = jnp.sum(jax.nn.one_hot(gidx, N_GROUPS, dtype=jnp.float32), axis=1) > 0
    emask = jnp.repeat(gmask, N_EXPERTS // N_GROUPS, axis=1)
    _, idx = lax.top_k(jnp.where(emask, sb, -jnp.inf), TOP_K)
    gate = jnp.take_along_axis(s, idx, axis=1)
    gate = gate / jnp.sum(gate, axis=-1, keepdims=True) * ROUTED_SCALE
    N = T * TOP_K
    e_flat = idx.reshape(N)
    tok = jnp.arange(N) // TOP_K
    w_flat = gate.reshape(N)
    order = jnp.argsort(e_flat)
    se = e_flat[order]
    counts = jnp.bincount(e_flat, length=N_EXPERTS)
    starts = jnp.cumsum(counts) - counts
    padded = (counts + MOE_BLOCK - 1) // MOE_BLOCK * MOE_BLOCK
    pends = jnp.cumsum(padded)
    pstarts = pends - padded
    dest = pstarts[se] + jnp.arange(N) - starts[se]
    nb = (N + N_EXPERTS * (MOE_BLOCK - 1) + MOE_BLOCK - 1) // MOE_BLOCK
    NP = nb * MOE_BLOCK
    row_tok = jnp.full((NP,), T, jnp.int32).at[dest].set(tok[order].astype(jnp.int32))
    row_w = jnp.zeros((NP,), jnp.float32).at[dest].set(w_flat[order])
    block_e = jnp.minimum(jnp.searchsorted(pends, jnp.arange(nb) * MOE_BLOCK, side='right'), N_EXPERTS - 1)
    x_pad = jnp.concatenate([xt, jnp.zeros((1, D), xt.dtype)], axis=0)

    def run_block(args):
        rows, e = args
        xb = x_pad[rows]
        hb = jax.nn.silu(xb @ wg[e]) * (xb @ wu[e])
        return hb @ wd[e]

    y_rows = lax.map(run_block, (row_tok.reshape(nb, MOE_BLOCK), block_e))
    y = jax.ops.segment_sum(y_rows.reshape(NP, D) * row_w[:, None].astype(x.dtype), row_tok,
                            num_segments=T + 1)[:T]
    shared = (jax.nn.silu(xt @ swg) * (xt @ swu)) @ swd
    return (y + shared).reshape(B, S, D)


def encoder_layer(x, c, l, lam_init, ada_w, ada_b, norm1_g, norm2_g, w_in, w_out, qk_norm_g, diff_lambda,
                  attn_out_g, hy_conv_w, hy_conv_b, hy_f_w1, hy_f_b1, hy_f_freq, hy_f_w2, hy_f_b2, hy_f_w3,
                  hy_f_b3, hy_bias, hy_out_g, ml_gate_b, ml_out_g, router_w, router_b, exp_w_gate, exp_w_up,
                  exp_w_down, sh_w_gate, sh_w_up, sh_w_down):
    B, L, D = x.shape
    mod = jax.nn.silu(c.astype(jnp.float32)) @ ada_w[l].astype(jnp.float32) + ada_b[l].astype(jnp.float32)
    mod = mod.reshape(B, N_MOD, 1, D).astype(x.dtype)
    filt = hyena_filter_fft(L, hy_f_w1[l], hy_f_b1[l], hy_f_freq[l], hy_f_w2[l], hy_f_b2[l], hy_f_w3[l], hy_f_b3[l])
    h = rms_norm(x, norm1_g[l]) * (1 + mod[:, 1]) + mod[:, 0]
    x = x + mod[:, 2] * mixer(h, w_in[l], w_out[l], qk_norm_g[l], diff_lambda[l], attn_out_g[l], hy_conv_w[l],
                              hy_conv_b[l], filt, hy_bias[l], hy_out_g[l], ml_gate_b[l], ml_out_g[l], lam_init)
    h = rms_norm(x, norm2_g[l]) * (1 + mod[:, 4]) + mod[:, 3]
    x = x + mod[:, 5] * moe(h, router_w[l], router_b[l], exp_w_gate[l], exp_w_up[l], exp_w_down[l],
                            sh_w_gate[l], sh_w_up[l], sh_w_down[l])
    return x


def kernel(x_prompt, x_sample, c_prompt, c_sample, ada_w, ada_b, norm1_g, norm2_g, w_in, w_out, qk_norm_g,
           diff_lambda, attn_out_g, hy_conv_w, hy_conv_b, hy_f_w1, hy_f_b1, hy_f_freq, hy_f_w2, hy_f_b2,
           hy_f_w3, hy_f_b3, hy_bias, hy_out_g, ml_gate_b, ml_out_g, router_w, router_b, exp_w_gate,
           exp_w_up, exp_w_down, sh_w_gate, sh_w_up, sh_w_down):
    args = (ada_w, ada_b, norm1_g, norm2_g, w_in, w_out, qk_norm_g, diff_lambda, attn_out_g, hy_conv_w,
            hy_conv_b, hy_f_w1, hy_f_b1, hy_f_freq, hy_f_w2, hy_f_b2, hy_f_w3, hy_f_b3, hy_bias, hy_out_g,
            ml_gate_b, ml_out_g, router_w, router_b, exp_w_gate, exp_w_up, exp_w_down, sh_w_gate, sh_w_up,
            sh_w_down)
    y_prompt = x_prompt
    y_sample = x_sample
    for l in range(DEPTH):
        lam_init = 0.8 - 0.6 * math.exp(-0.3 * l)
        y_prompt = encoder_layer(y_prompt, c_prompt, l, lam_init, *args)
        y_sample = encoder_layer(y_sample, c_sample, l, lam_init, *args)
    return (y_prompt, y_sample)
```

```python
import functools
import math

import jax
import jax.numpy as jnp
import numpy as np
from jax import lax
from jax.experimental import pallas as pl
from jax.experimental.pallas import tpu as pltpu

DEPTH = 2
ATT_QK = 64
ATT_V = 2 * ATT_QK
ATT_HEADS = 12
ATT_W = ATT_HEADS * ATT_V
HY_W = 1024
HY_ORDER = 2
HY_GROUPS = 8
HY_EMB = 33
HY_TARGET = 1e-2
HY_FAST = 0.3
HY_SLOW = 1.5
ML_V = 256
ML_QK = ML_V // 2
ML_HEADS = 6
ML_W = ML_HEADS * ML_V
N_EXPERTS = 64
TOP_K = 8
N_GROUPS = 8
TOPK_GROUPS = 4
ROUTED_SCALE = 2.5
N_MOD = 6
EPS = 1e-6

LANES = 128
VMEM_LIMIT = 56 * 1024 * 1024

ML_CHUNK = 256
TB_MOE = 512
TF_MOE = 256
TM_MM = 1024
TN_IN = 768
TN_OUT = 512
TM_ROW = 256
TQ_ATT = 1024
TK_ATT = 1024
TM_DFT = 512
TN_DFT = 256

f32 = jnp.float32
bf16 = jnp.bfloat16
HI = lax.Precision.HIGHEST


def _cparams(*sem):
    return pltpu.CompilerParams(dimension_semantics=sem, vmem_limit_bytes=VMEM_LIMIT)


class _Layout:
    def __init__(self, nb1, s1, nb2, s2):
        self.nb1, self.s1, self.nb2, self.s2 = nb1, s1, nb2, s2
        self.p = nb1 * s1
        self.t = self.p + nb2 * s2
        self.nseq = nb1 + nb2

    def seq_of_row(self, r):
        return jnp.where(r < self.p, r // self.s1, self.nb1 + (r - self.p) // self.s2)

    def is_boundary(self, r):
        return jnp.where(r <= self.p, r % self.s1 == 0, (r - self.p) % self.s2 == 0)

    def groups(self):
        return ((0, self.s1, self.nb1), (self.p, self.s2, self.nb2))


def _ada_body(c_ref, w_ref, b_ref, o_ref):
    c = c_ref[...]
    s = c * jax.nn.sigmoid(c)
    o_ref[...] = jnp.dot(s, w_ref[...], preferred_element_type=f32, precision=HI) + b_ref[...]


def ada_modulation(c_pad, ada_w, ada_b, tn):
    r, d = c_pad.shape
    n = ada_w.shape[2]
    return pl.pallas_call(
        _ada_body,
        grid=(DEPTH, n // tn),
        in_specs=[pl.BlockSpec((r, d), lambda l, j: (0, 0)),
                  pl.BlockSpec((None, d, tn), lambda l, j: (l, 0, j)),
                  pl.BlockSpec((None, 1, tn), lambda l, j: (l, 0, j))],
        out_specs=pl.BlockSpec((None, r, tn), lambda l, j: (l, 0, j)),
        out_shape=jax.ShapeDtypeStruct((DEPTH, r, n), f32),
        compiler_params=_cparams("parallel", "parallel"),
        name="ada_modulation",
    )(c_pad, ada_w, ada_b.reshape(DEPTH, 1, n))


def _norm_mod_body(x_ref, mod_ref, g_ref, *rest, shift_idx, scale_idx, with_router):
    x = x_ref[...]
    ms = jnp.mean(x * x, axis=-1, keepdims=True)
    y = x * lax.rsqrt(ms + EPS) * g_ref[...]
    h = y * (1.0 + mod_ref[scale_idx:scale_idx + 1, :]) + mod_ref[shift_idx:shift_idx + 1, :]
    if with_router:
        rw_ref, h_ref, logit_ref = rest
        logit_ref[...] = jnp.dot(h, rw_ref[...], preferred_element_type=f32, precision=HI)
    else:
        (h_ref,) = rest
    h_ref[...] = h.astype(bf16)


def norm_modulate(x, mod, gain, l, lay, shift_idx, scale_idx, router_w=None):
    t, d = x.shape
    tm = TM_ROW
    seq = lambda i: lay.seq_of_row(i * tm)
    in_specs = [pl.BlockSpec((tm, d), lambda i: (i, 0)),
                pl.BlockSpec((None, None, N_MOD, d), lambda i: (l, seq(i), 0, 0)),
                pl.BlockSpec((None, 1, d), lambda i: (l, 0, 0))]
    args = [x, mod, gain.reshape(DEPTH, 1, d)]
    out_specs = [pl.BlockSpec((tm, d), lambda i: (i, 0))]
    out_shape = [jax.ShapeDtypeStruct((t, d), bf16)]
    if router_w is not None:
        e = router_w.shape[2]
        in_specs.append(pl.BlockSpec((None, d, e), lambda i: (l, 0, 0)))
        args.append(router_w)
        out_specs.append(pl.BlockSpec((tm, e), lambda i: (i, 0)))
        out_shape.append(jax.ShapeDtypeStruct((t, e), f32))
    return pl.pallas_call(
        functools.partial(_norm_mod_body, shift_idx=shift_idx, scale_idx=scale_idx,
                          with_router=router_w is not None),
        grid=(t // tm,), in_specs=in_specs, out_specs=out_specs, out_shape=out_shape,
        compiler_params=_cparams("parallel"),
        name="norm_modulate",
    )(*args)


def _mm_body(a_ref, b_ref, o_ref):
    o_ref[...] = jnp.dot(a_ref[...], b_ref[...], preferred_element_type=f32).astype(o_ref.dtype)


def matmul(a, b, l, n_out, tm, tn, out_dtype=f32):
    m, k = a.shape
    return pl.pallas_call(
        _mm_body,
        grid=(m // tm, n_out // tn),
        in_specs=[pl.BlockSpec((tm, k), lambda i, j: (i, 0)),
                  pl.BlockSpec((None, k, tn), lambda i, j: (l, 0, j))],
        out_specs=pl.BlockSpec((tm, tn), lambda i, j: (i, j)),
        out_shape=jax.ShapeDtypeStruct((m, n_out), out_dtype),
        compiler_params=_cparams("parallel", "parallel"),
        name="matmul",
    )(a, b)


def _mm_res_body(a_ref, b_ref, res_ref, mod_ref, o_ref, *, gate_idx):
    acc = jnp.dot(a_ref[...], b_ref[...], preferred_element_type=f32)
    o_ref[...] = res_ref[...] + mod_ref[gate_idx:gate_idx + 1, :] * acc


def matmul_gated_residual(a, b, l, res, mod, lay, gate_idx, tm, tn):
    m, k = a.shape
    n = res.shape[1]
    seq = lambda i: lay.seq_of_row(i * tm)
    return pl.pallas_call(
        functools.partial(_mm_res_body, gate_idx=gate_idx),
        grid=(m // tm, n // tn),
        in_specs=[pl.BlockSpec((tm, k), lambda i, j: (i, 0)),
                  pl.BlockSpec((None, k, tn), lambda i, j: (l, 0, j)),
                  pl.BlockSpec((tm, tn), lambda i, j: (i, j)),
                  pl.BlockSpec((None, None, N_MOD, tn), lambda i, j: (l, seq(i), 0, j))],
        out_specs=pl.BlockSpec((tm, tn), lambda i, j: (i, j)),
        out_shape=jax.ShapeDtypeStruct((m, n), f32),
        compiler_params=_cparams("parallel", "parallel"),
        name="matmul_gated_residual",
    )(a, b, res, mod)


def _attn_pre_body(q_ref, k_ref, v_ref, g_ref, qa_ref, qb_ref, kn_ref, vb_ref):
    lane = lax.broadcasted_iota(jnp.int32, q_ref.shape, 1)
    lo = lane < ATT_QK

    def half_norm(x, g):
        sq = x * x
        s_lo = jnp.sum(jnp.where(lo, sq, 0.0), axis=-1, keepdims=True)
        s_hi = jnp.sum(jnp.where(lo, 0.0, sq), axis=-1, keepdims=True)
        inv = jnp.where(lo, lax.rsqrt(s_lo / ATT_QK + EPS), lax.rsqrt(s_hi / ATT_QK + EPS))
        return x * inv * g

    qn = half_norm(q_ref[...], g_ref[0:1, :]) * (ATT_QK ** -0.5)
    qa_ref[...] = jnp.where(lo, qn, 0.0).astype(bf16)
    qb_ref[...] = jnp.where(lo, 0.0, qn).astype(bf16)
    kn_ref[...] = half_norm(k_ref[...], g_ref[1:2, :]).astype(bf16)
    vb_ref[...] = v_ref[...].astype(bf16)


def attn_prepare(proj, qk_g2, tm):
    t = proj.shape[0]
    hd = ATT_HEADS
    blk = lambda off: pl.BlockSpec((tm, ATT_V), lambda i, h: (i, off + h))
    out_spec = pl.BlockSpec((None, tm, ATT_V), lambda i, h: (h, i, 0))
    shp = jax.ShapeDtypeStruct((hd, t, ATT_V), bf16)
    return pl.pallas_call(
        _attn_pre_body,
        grid=(t // tm, hd),
        in_specs=[blk(0), blk(hd), blk(2 * hd), pl.BlockSpec((2, ATT_V), lambda i, h: (0, 0))],
        out_specs=[out_spec] * 4, out_shape=[shp] * 4,
        compiler_params=_cparams("parallel", "parallel"),
        name="attn_prepare",
    )(proj, proj, proj, qk_g2)


def _attn_body(scal_ref, qa_ref, qb_ref, k_ref, v_ref, g_ref, o_ref, m_ref, l_ref, acc_ref,
               *, tq, tk, seq_len, out_scale):
    qi, h, kb = pl.program_id(0), pl.program_id(1), pl.program_id(2)

    @pl.when(kb == 0)
    def _():
        m_ref[...] = jnp.full(m_ref.shape, -jnp.inf, f32)
        l_ref[...] = jnp.zeros(l_ref.shape, f32)
        acc_ref[...] = jnp.zeros(acc_ref.shape, f32)

    d0 = (qi * tq) % seq_len - kb * tk
    rc = lax.broadcasted_iota(jnp.int32, (tq, tk), 0) - lax.broadcasted_iota(jnp.int32, (tq, tk), 1)
    bias = jnp.abs(rc + d0).astype(f32) * (-scal_ref[h])
    k = k_ref[...]
    v = v_ref[...]
    for mi, q_ref in enumerate((qa_ref, qb_ref)):
        s = lax.dot_general(q_ref[...], k, (((1,), (1,)), ((), ())), preferred_element_type=f32) + bias
        m_prev = m_ref[mi]
        m_new = jnp.maximum(m_prev, jnp.max(s, axis=-1, keepdims=True))
        alpha = jnp.exp(m_prev - m_new)
        p = jnp.exp(s - m_new)
        l_ref[mi] = alpha * l_ref[mi] + jnp.sum(p, axis=-1, keepdims=True)
        acc_ref[mi] = alpha * acc_ref[mi] + jnp.dot(p.astype(bf16), v, preferred_element_type=f32)
        m_ref[mi] = m_new

    @pl.when(kb == pl.num_programs(2) - 1)
    def _():
        lam = scal_ref[ATT_HEADS]
        o = acc_ref[0] / l_ref[0] - lam * (acc_ref[1] / l_ref[1])
        ms = jnp.mean(o * o, axis=-1, keepdims=True)
        o_ref[...] = (o * lax.rsqrt(ms + EPS) * g_ref[...] * out_scale).astype(o_ref.dtype)


def attention_group(scal, qa, qb, kn, vb, out_g, row_start, seq_len, nseq, tq, tk, out_scale):
    tq, tk = min(tq, seq_len), min(tk, seq_len)
    nkb = seq_len // tk
    q0, k0 = row_start // tq, row_start // tk
    qspec = pl.BlockSpec((None, tq, ATT_V), lambda qi, h, kb: (h, q0 + qi, 0))
    kspec = pl.BlockSpec((None, tk, ATT_V), lambda qi, h, kb: (h, k0 + (qi * tq) // seq_len * nkb + kb, 0))
    rows = nseq * seq_len
    return pl.pallas_call(
        functools.partial(_attn_body, tq=tq, tk=tk, seq_len=seq_len, out_scale=out_scale),
        grid=(rows // tq, ATT_HEADS, nkb),
        in_specs=[pl.BlockSpec(memory_space=pltpu.SMEM), qspec, qspec, kspec, kspec,
                  pl.BlockSpec((1, ATT_V), lambda qi, h, kb: (0, 0))],
        out_specs=pl.BlockSpec((tq, ATT_V), lambda qi, h, kb: (qi, h)),
        out_shape=jax.ShapeDtypeStruct((rows, ATT_W), bf16),
        scratch_shapes=[pltpu.VMEM((2, tq, 1), f32), pltpu.VMEM((2, tq, 1), f32),
                        pltpu.VMEM((2, tq, ATT_V), f32)],
        compiler_params=_cparams("parallel", "parallel", "arbitrary"),
        name="diff_attention",
    )(scal, qa, qb, kn, vb, out_g)


def _mlstm_body(q_ref, k_ref, v_ref, gc_ref, gr_ref, h_ref, c_ref, n_ref, m_ref, *, lc, lay, backward):
    c = pl.program_id(0)
    nchunks = pl.num_programs(0)
    heads = ML_HEADS
    ceff = nchunks - 1 - c if backward else c
    r0 = ceff * lc
    reset = lay.is_boundary(r0 + lc) if backward else lay.is_boundary(r0)

    @pl.when(reset)
    def _():
        c_ref[...] = jnp.zeros(c_ref.shape, f32)
        n_ref[...] = jnp.zeros(n_ref.shape, f32)
        m_ref[...] = jnp.zeros(m_ref.shape, f32)

    row = lax.broadcasted_iota(jnp.int32, (lc, lc), 0)
    col = lax.broadcasted_iota(jnp.int32, (lc, lc), 1)
    mask = (col >= row) if backward else (col <= row)
    gcol = gc_ref[...]
    lane = lax.broadcasted_iota(jnp.int32, gcol.shape, 1)
    d = 1 if backward else 0

    def col_of(idx):
        return jnp.sum(jnp.where(lane == idx, gcol, 0.0), axis=-1, keepdims=True)

    for h in range(heads):
        ia = d * heads + h
        a_col, i_col = col_of(ia), col_of(2 * heads + ia)
        a_row = gr_ref[ia:ia + 1, :]
        i_row = gr_ref[2 * heads + ia:2 * heads + ia + 1, :]
        g11 = gr_ref[4 * heads + ia:4 * heads + ia + 1, 0:1]
        m11 = m_ref[h][:, 0:1]
        qf = q_ref[:, h * ML_QK:(h + 1) * ML_QK] * (ML_QK ** -0.5)
        kf = k_ref[:, h * ML_QK:(h + 1) * ML_QK]
        qb, kb = qf.astype(bf16), kf.astype(bf16)
        vb = v_ref[:, h * ML_V:(h + 1) * ML_V].astype(bf16)
        dmat = jnp.where(mask, a_col - a_row + i_row, -jnp.inf)
        m_inter = a_col + m11
        mj = jnp.maximum(jnp.max(dmat, axis=-1, keepdims=True), m_inter)
        p = jnp.exp(dmat - mj)
        s = lax.dot_general(qb, kb, (((1,), (1,)), ((), ())), preferred_element_type=f32)
        wts = p * s
        inter = jnp.exp(m_inter - mj)
        cmat = c_ref[h]
        num = (jnp.dot(wts.astype(bf16), vb, preferred_element_type=f32)
               + inter * jnp.dot(qb, cmat.astype(bf16), preferred_element_type=f32))
        nvec = n_ref[h]
        den = jnp.sum(wts, axis=-1, keepdims=True) + inter * jnp.sum(qf * nvec, axis=-1, keepdims=True)
        h_ref[:, h * ML_V:(h + 1) * ML_V] = num / jnp.maximum(jnp.abs(den), jnp.exp(-mj))
        ws = g11 - a_col + i_col
        m_new = jnp.maximum(g11 + m11, jnp.max(ws, axis=0, keepdims=True))
        dec = jnp.exp(g11 + m11 - m_new)
        ek = jnp.exp(ws - m_new) * kf
        c_ref[h] = dec * cmat + lax.dot_general(ek.astype(bf16), vb, (((0,), (0,)), ((), ())),
                                                preferred_element_type=f32)
        n_ref[h] = dec * nvec + jnp.sum(ek, axis=0, keepdims=True)
        m_ref[h] = jnp.broadcast_to(m_new, (1, LANES))


def mlstm_scan(proj, gcol, grow, lay, lc, backward, q_off, k_off, v_off):
    t = proj.shape[0]
    nchunks = t // lc
    hq, hv = ML_HEADS * ML_QK, ML_W
    cidx = (lambda c: nchunks - 1 - c) if backward else (lambda c: c)
    return pl.pallas_call(
        functools.partial(_mlstm_body, lc=lc, lay=lay, backward=backward),
        grid=(nchunks,),
        in_specs=[pl.BlockSpec((lc, hq), lambda c: (cidx(c), q_off // hq)),
                  pl.BlockSpec((lc, hq), lambda c: (cidx(c), k_off // hq)),
                  pl.BlockSpec((lc, hv), lambda c: (cidx(c), v_off // hv)),
                  pl.BlockSpec((lc, LANES), lambda c: (cidx(c), 0)),
                  pl.BlockSpec((grow.shape[0], lc), lambda c: (0, cidx(c)))],
        out_specs=pl.BlockSpec((lc, hv), lambda c: (cidx(c), 0)),
        out_shape=jax.ShapeDtypeStruct((t, hv), f32),
        scratch_shapes=[pltpu.VMEM((ML_HEADS, ML_QK, ML_V), f32), pltpu.VMEM((ML_HEADS, 1, ML_QK), f32),
                        pltpu.VMEM((ML_HEADS, 1, LANES), f32)],
        compiler_params=_cparams("arbitrary"),
        name="mlstm_bwd" if backward else "mlstm_fwd",
    )(proj, proj, proj, gcol, grow)


def _mlstm_post_body(hf_ref, hb_ref, o_ref, g_ref, y_ref):
    hsum = hf_ref[...] + hb_ref[...]
    ms = jnp.mean(hsum * hsum, axis=-1, keepdims=True)
    y_ref[...] = (jax.nn.sigmoid(o_ref[...]) * (hsum * lax.rsqrt(ms + EPS) * g_ref[...])).astype(y_ref.dtype)


def mlstm_post(hf, hb, proj, o_off, ml_g, l, tm):
    t = hf.shape[0]
    blk = lambda off: pl.BlockSpec((tm, ML_V), lambda i, h: (i, off + h))
    return pl.pallas_call(
        _mlstm_post_body,
        grid=(t // tm, ML_HEADS),
        in_specs=[blk(0), blk(0), blk(o_off // ML_V), pl.BlockSpec((None, 1, ML_V), lambda i, h: (l, 0, h))],
        out_specs=blk(0),
        out_shape=jax.ShapeDtypeStruct((t, ML_W), bf16),
        compiler_params=_cparams("parallel", "parallel"),
        name="mlstm_post",
    )(hf, hb, proj, ml_g.reshape(DEPTH, 1, ML_W))


def mlstm_gate_tables(gates, gate_b, lc):
    t = gates.shape[0]
    hd = ML_HEADS
    g = gates[:, :4 * hd].reshape(t, 4, hd) + gate_b.astype(f32)
    i_f, f_f, i_b, f_b = g[:, 0], g[:, 1], g[:, 2], g[:, 3]
    lf_f = jax.nn.log_sigmoid(f_f).reshape(t // lc, lc, hd)
    lf_b = jax.nn.log_sigmoid(f_b).reshape(t // lc, lc, hd)
    a_f = jnp.cumsum(lf_f, axis=1)
    a_b = jnp.flip(jnp.cumsum(jnp.flip(lf_b, axis=1), axis=1), axis=1)
    tot_f = jnp.broadcast_to(a_f[:, -1:, :], a_f.shape)
    tot_b = jnp.broadcast_to(a_b[:, :1, :], a_b.shape)
    flat = lambda a: a.reshape(t, hd)
    cols = jnp.concatenate([flat(a_f), flat(a_b), i_f, i_b, flat(tot_f), flat(tot_b)], axis=1)
    gcol = jnp.pad(cols, ((0, 0), (0, LANES - cols.shape[1])))
    grow = jnp.pad(cols.T, ((0, (-cols.shape[1]) % 8), (0, 0)))
    return gcol, grow


def _hyconv_body(c_ref, p_ref, n_ref, w_ref, b_ref, u_ref, ub_ref, *, tm, lay):
    r0 = pl.program_id(0) * tm
    x = c_ref[...]
    prev_row = jnp.where(lay.is_boundary(r0), 0.0, p_ref[7:8, :])
    next_row = jnp.where(lay.is_boundary(r0 + tm), 0.0, n_ref[0:1, :])
    rid = lax.broadcasted_iota(jnp.int32, x.shape, 0)
    xm = jnp.where(rid == 0, prev_row, pltpu.roll(x, 1, axis=0))
    xp = jnp.where(rid == tm - 1, next_row, pltpu.roll(x, tm - 1, axis=0))
    u = xm * w_ref[0:1, :] + x * w_ref[1:2, :] + xp * w_ref[2:3, :] + b_ref[...]
    u_ref[...] = u
    ub_ref[...] = u.astype(bf16)


def hyena_short_conv(proj, hy_off, conv_w, conv_b, l, lay, tm, tc):
    t = proj.shape[0]
    width = 3 * HY_W
    c0 = hy_off // tc
    nblk8 = t // 8
    return pl.pallas_call(
        functools.partial(_hyconv_body, tm=tm, lay=lay),
        grid=(t // tm, width // tc),
        in_specs=[pl.BlockSpec((tm, tc), lambda i, j: (i, c0 + j)),
                  pl.BlockSpec((8, tc), lambda i, j: (jnp.maximum(i * (tm // 8) - 1, 0), c0 + j)),
                  pl.BlockSpec((8, tc), lambda i, j: (jnp.minimum((i + 1) * (tm // 8), nblk8 - 1), c0 + j)),
                  pl.BlockSpec((None, 3, tc), lambda i, j: (l, 0, j)),
                  pl.BlockSpec((None, 1, tc), lambda i, j: (l, 0, j))],
        out_specs=[pl.BlockSpec((tm, tc), lambda i, j: (i, j))] * 2,
        out_shape=[jax.ShapeDtypeStruct((t, width), f32), jax.ShapeDtypeStruct((t, width), bf16)],
        compiler_params=_cparams("parallel", "parallel"),
        name="hyena_short_conv",
    )(proj, proj, proj, conv_w, conv_b.reshape(DEPTH, 1, width))


def hyena_time_filters(seq_len, w1, b1, freq, w2, b2, w3, b3):
    t = jnp.linspace(0.0, 1.0, seq_len, dtype=f32)[:, None]
    bands = (HY_EMB - 1) // 2
    fr = jnp.linspace(1e-4, bands - 1, bands, dtype=f32)
    ang = (2.0 * math.pi / seq_len) * jnp.arange(seq_len, dtype=f32)[:, None] * fr[None, :]
    z = jnp.concatenate([t, jnp.cos(ang), -jnp.sin(ang)], axis=-1)
    fq = freq.astype(f32)
    h = jnp.sin(fq[0] * (jnp.dot(z, w1, precision=HI) + b1))
    h = jnp.sin(fq[1] * (jnp.dot(h, w2, precision=HI) + b2))
    h = (jnp.dot(h, w3, precision=HI) + b3).reshape(seq_len, HY_ORDER, 2, HY_W)
    deltas = jnp.linspace(math.log(HY_TARGET) / HY_SLOW, math.log(HY_TARGET) / HY_FAST, HY_W, dtype=f32)
    h = h * jnp.exp(-t * jnp.abs(deltas))[:, None, None, :]
    hf, hb = h[:, :, 0], h[:, :, 1]
    first = (jnp.arange(seq_len) == 0)[:, None, None]
    hf = jnp.where(first, hf + hb, hf)
    hb = jnp.where(first, 0.0, hb)
    p = (hf + hb).reshape(seq_len, HY_ORDER * HY_W)
    q = (hf - hb).reshape(seq_len, HY_ORDER * HY_W)
    return p.astype(bf16), q.astype(bf16)


def dft_matrices(seq_len):
    idx = jnp.arange(seq_len, dtype=jnp.int32)
    r = (idx[:, None] * idx[None, :]) % (2 * seq_len)
    ang = r.astype(f32) * (math.pi / seq_len)
    cos_m = jnp.cos(ang)
    alt = jnp.where(idx % 2 == 0, 1.0, -1.0).astype(f32)
    sin_m = jnp.where(idx[:, None] == 0, alt[None, :], -jnp.sin(ang))
    return cos_m.astype(bf16), sin_m.astype(bf16), sin_m.T.astype(bf16)


def _hy_filter_body(fc_ref, fs_ref, p_ref, q_ref, hr_ref, hi_ref):
    p = p_ref[...]
    hr_ref[...] = jnp.dot(fc_ref[...], p, preferred_element_type=f32)
    hi = jnp.dot(fs_ref[...], q_ref[...], preferred_element_type=f32)
    nyq = jnp.dot(fs_ref[0:8, :], p, preferred_element_type=f32)[0:1, :]
    grow = pl.program_id(0) * hr_ref.shape[0] + lax.broadcasted_iota(jnp.int32, hi.shape, 0)
    hi_ref[...] = jnp.where(grow == 0, nyq, hi)


def hyena_filter_spectrum(cos_m, sin_m, p, q, tm, tn):
    seq_len, n = p.shape
    a_spec = pl.BlockSpec((tm, seq_len), lambda i, j: (i, 0))
    b_spec = pl.BlockSpec((seq_len, tn), lambda i, j: (0, j))
    o_spec = pl.BlockSpec((tm, tn), lambda i, j: (i, j))
    shp = jax.ShapeDtypeStruct((seq_len, n), f32)
    return pl.pallas_call(
        _hy_filter_body,
        grid=(seq_len // tm, n // tn),
        in_specs=[a_spec, a_spec, b_spec, b_spec], out_specs=[o_spec, o_spec], out_shape=[shp, shp],
        compiler_params=_cparams("parallel", "parallel"),
        name="hyena_filter_spectrum",
    )(cos_m, sin_m, p, q)


def _hy_fwd_body(fc_ref, fs_ref, z_ref, hr_ref, hi_ref, yr_ref, yi_ref, *, n_fft):
    z = z_ref[...]
    vr = jnp.dot(fc_ref[...], z, preferred_element_type=f32)
    vi = jnp.dot(fs_ref[...], z, preferred_element_type=f32)
    hr, hi = hr_ref[...], hi_ref[...]
    grow = pl.program_id(0) * vr.shape[0] + lax.broadcasted_iota(jnp.int32, vr.shape, 0)
    row0 = grow == 0
    vihi = vi * hi
    yr = vr * hr - jnp.where(row0, 0.0, vihi)
    yi = jnp.where(row0, vihi, vr * hi + vi * hr)
    scale = jnp.where(row0, 1.0 / n_fft, 2.0 / n_fft)
    yr_ref[...] = (yr * scale).astype(bf16)
    yi_ref[...] = (yi * scale).astype(bf16)


def hyena_fwd(cos_m, sin_m, zb, z_col, hr, hi, order, row_start, seq_len, nseq, tm, tn):
    tm, tn = min(tm, seq_len), min(tn, HY_W)
    rows = nseq * seq_len
    s0, i0 = row_start // seq_len, seq_len // tm
    a_spec = pl.BlockSpec((tm, seq_len), lambda i, s, j: (i, 0))
    h_spec = pl.BlockSpec((tm, tn), lambda i, s, j: (i, order * (HY_W // tn) + j))
    o_spec = pl.BlockSpec((tm, tn), lambda i, s, j: (s * i0 + i, j))
    shp = jax.ShapeDtypeStruct((rows, HY_W), bf16)
    return pl.pallas_call(
        functools.partial(_hy_fwd_body, n_fft=2 * seq_len),
        grid=(seq_len // tm, nseq, HY_W // tn),
        in_specs=[a_spec, a_spec,
                  pl.BlockSpec((seq_len, tn), lambda i, s, j: (s0 + s, z_col // tn + j)),
                  h_spec, h_spec],
        out_specs=[o_spec, o_spec], out_shape=[shp, shp],
        compiler_params=_cparams("parallel", "parallel", "parallel"),
        name="hyena_fwd_dft",
    )(cos_m, sin_m, zb, hr, hi)


def _hy_inv_body(fc_ref, ft_ref, yr_ref, yi_ref, z_ref, x_ref, d_ref, *rest, last):
    y = (jnp.dot(fc_ref[...], yr_ref[...], preferred_element_type=f32)
         + jnp.dot(ft_ref[...], yi_ref[...], preferred_element_type=f32))
    z = x_ref[...] * (y + d_ref[...] * z_ref[...])
    if last:
        g_ref, o_ref = rest
        gw = HY_W // HY_GROUPS
        for gi in range(z.shape[1] // gw):
            seg = z[:, gi * gw:(gi + 1) * gw]
            ms = jnp.mean(seg * seg, axis=-1, keepdims=True)
            o_ref[:, gi * gw:(gi + 1) * gw] = (seg * lax.rsqrt(ms + EPS)
                                               * g_ref[:, gi * gw:(gi + 1) * gw]).astype(o_ref.dtype)
    else:
        o_ref, ob_ref = rest
        o_ref[...] = z
        ob_ref[...] = z.astype(bf16)


def hyena_inv(cos_m, sin_t, yr, yi, z_arr, z_row0, z_col, u, x_col, d_bias, gain, l, order,
              row_start, seq_len, nseq, tm, tn):
    tm, tn = min(tm, seq_len), min(tn, HY_W)
    last = order == HY_ORDER - 1
    rows = nseq * seq_len
    i0 = seq_len // tm
    a_spec = pl.BlockSpec((tm, seq_len), lambda i, s, j: (i, 0))
    b_spec = pl.BlockSpec((seq_len, tn), lambda i, s, j: (s, j))
    e_spec = lambda row0, col: pl.BlockSpec((tm, tn), lambda i, s, j: (row0 // tm + s * i0 + i, col // tn + j))
    o_spec = pl.BlockSpec((tm, tn), lambda i, s, j: (s * i0 + i, j))
    in_specs = [a_spec, a_spec, b_spec, b_spec, e_spec(z_row0, z_col), e_spec(row_start, x_col),
                pl.BlockSpec((None, 1, tn), lambda i, s, j: (l * HY_ORDER + order, 0, j))]
    args = [cos_m, sin_t, yr, yi, z_arr, u, d_bias.reshape(DEPTH * HY_ORDER, 1, HY_W)]
    if last:
        in_specs.append(pl.BlockSpec((None, 1, tn), lambda i, s, j: (l, 0, j)))
        args.append(gain.reshape(DEPTH, 1, HY_W))
        out_specs = o_spec
        out_shape = jax.ShapeDtypeStruct((rows, HY_W), bf16)
    else:
        out_specs = [o_spec, o_spec]
        out_shape = [jax.ShapeDtypeStruct((rows, HY_W), f32), jax.ShapeDtypeStruct((rows, HY_W), bf16)]
    return pl.pallas_call(
        functools.partial(_hy_inv_body, last=last),
        grid=(seq_len // tm, nseq, HY_W // tn),
        in_specs=in_specs, out_specs=out_specs, out_shape=out_shape,
        compiler_params=_cparams("parallel", "parallel", "parallel"),
        name="hyena_inv_dft",
    )(*args)


def hyena_group(u, ub, spec, d_bias, gain, l, row_start, seq_len, nseq):
    cos_m, sin_m, sin_t, hr, hi = spec
    grp = (row_start, seq_len, nseq, TM_DFT, TN_DFT)
    yr, yi = hyena_fwd(cos_m, sin_m, ub, 0, hr, hi, 0, *grp)
    z1, z1b = hyena_inv(cos_m, sin_t, yr, yi, u, row_start, 0, u, HY_W, d_bias, gain, l, 0, *grp)
    yr, yi = hyena_fwd(cos_m, sin_m, z1b, 0, hr, hi, 1, 0, seq_len, nseq, TM_DFT, TN_DFT)
    return hyena_inv(cos_m, sin_t, yr, yi, z1, 0, 0, u, 2 * HY_W, d_bias, gain, l, 1, *grp)


def _ffn_body(be_ref, nused_ref, x_ref, wg_ref, wu_ref, wd_ref, o_ref, acc_ref):
    b, f = pl.program_id(0), pl.program_id(1)

    @pl.when(b < nused_ref[0])
    def _():
        x = x_ref[...]
        g = jnp.dot(x, wg_ref[...], preferred_element_type=f32)
        u = jnp.dot(x, wu_ref[...], preferred_element_type=f32)
        hmid = (g * jax.nn.sigmoid(g) * u).astype(bf16)
        part = jnp.dot(hmid, wd_ref[...], preferred_element_type=f32)

        @pl.when(f == 0)
        def _():
            acc_ref[...] = part

        @pl.when(f > 0)
        def _():
            acc_ref[...] += part

    @pl.when(f == pl.num_programs(1) - 1)
    def _():
        o_ref[...] = acc_ref[...].astype(o_ref.dtype)


def grouped_ffn(x_rows, block_e, n_used, wg, wu, wd, l, tb, tf, out_dtype):
    rows, d = x_rows.shape
    fdim = wg.shape[3]
    nb = rows // tb
    live = lambda b, n: jnp.minimum(b, n[0] - 1)
    grid_spec = pltpu.PrefetchScalarGridSpec(
        num_scalar_prefetch=2,
        grid=(nb, fdim // tf),
        in_specs=[pl.BlockSpec((tb, d), lambda b, f, be, n: (live(b, n), 0)),
                  pl.BlockSpec((None, None, d, tf), lambda b, f, be, n: (l, be[live(b, n)], 0, f)),
                  pl.BlockSpec((None, None, d, tf), lambda b, f, be, n: (l, be[live(b, n)], 0, f)),
                  pl.BlockSpec((None, None, tf, d), lambda b, f, be, n: (l, be[live(b, n)], f, 0))],
        out_specs=pl.BlockSpec((tb, d), lambda b, f, be, n: (b, 0)),
        scratch_shapes=[pltpu.VMEM((tb, d), f32)],
    )
    return pl.pallas_call(
        _ffn_body, grid_spec=grid_spec,
        out_shape=jax.ShapeDtypeStruct((rows, d), out_dtype),
        compiler_params=_cparams("arbitrary", "arbitrary"),
        name="grouped_ffn",
    )(block_e, n_used, x_rows, wg, wu, wd)


def route_tokens(logits, router_b):
    t = logits.shape[0]
    per = N_EXPERTS // N_GROUPS
    s = jax.nn.sigmoid(logits)
    sb = s + router_b.astype(f32)
    gscore = jnp.sum(lax.top_k(sb.reshape(t, N_GROUPS, per), 2)[0], axis=-1)
    _, gidx = lax.top_k(gscore, TOPK_GROUPS)
    gmask = jnp.sum(jax.nn.one_hot(gidx, N_GROUPS, dtype=f32), axis=1) > 0
    emask = jnp.repeat(gmask, per, axis=1)
    _, idx = lax.top_k(jnp.where(emask, sb, -jnp.inf), TOP_K)
    gate = jnp.take_along_axis(s, idx, axis=1)
    gate = gate / jnp.sum(gate, axis=-1, keepdims=True) * ROUTED_SCALE
    return idx, gate


def dispatch_plan(idx, tb):
    t = idx.shape[0]
    n = t * TOP_K
    e_flat = idx.reshape(n).astype(jnp.int32)
    onehot = (e_flat[:, None] == jnp.arange(N_EXPERTS, dtype=jnp.int32)[None, :]).astype(jnp.int32)
    csum = jnp.cumsum(onehot, axis=0)
    counts = csum[-1]
    rank = jnp.take_along_axis(csum, e_flat[:, None], axis=1)[:, 0] - 1
    starts = jnp.cumsum(counts) - counts
    padded = (counts + tb - 1) // tb * tb
    pends = jnp.cumsum(padded)
    pstarts = pends - padded
    pos = (pstarts[e_flat] + rank).reshape(t, TOP_K)
    nb = (n + N_EXPERTS * (tb - 1) + tb - 1) // tb
    block_e = jnp.minimum(jnp.searchsorted(pends, jnp.arange(nb, dtype=jnp.int32) * tb, side='right'),
                          N_EXPERTS - 1).astype(jnp.int32)
    order = jnp.argsort(e_flat, stable=True).astype(jnp.int32)
    e_row = jnp.repeat(block_e, tb)
    off = jnp.arange(nb * tb, dtype=jnp.int32) - pstarts[e_row]
    valid = off < counts[e_row]
    src = jnp.where(valid, starts[e_row] + off, 0)
    row_tok = jnp.where(valid, order[src] // TOP_K, 0)
    n_used = (pends[-1:] // tb).astype(jnp.int32)
    return row_tok, pos, block_e, n_used


def _final_body(x_ref, moe_ref, sh_ref, mod_ref, o_ref, *, gate_idx):
    o_ref[...] = x_ref[...] + mod_ref[gate_idx:gate_idx + 1, :] * (moe_ref[...] + sh_ref[...])


def gated_residual_sum(x, moe_y, shared_y, mod, l, lay, gate_idx, tm):
    t, d = x.shape
    seq = lambda i: lay.seq_of_row(i * tm)
    blk = pl.BlockSpec((tm, d), lambda i: (i, 0))
    return pl.pallas_call(
        functools.partial(_final_body, gate_idx=gate_idx),
        grid=(t // tm,),
        in_specs=[blk, blk, blk, pl.BlockSpec((None, None, N_MOD, d), lambda i: (l, seq(i), 0, 0))],
        out_specs=blk, out_shape=jax.ShapeDtypeStruct((t, d), f32),
        compiler_params=_cparams("parallel"),
        name="gated_residual_sum",
    )(x, moe_y, shared_y, mod)


def _col_offsets():
    sizes = [ATT_W, ATT_W, ATT_W, 3 * HY_W, ML_HEADS * ML_QK, ML_HEADS * ML_QK, ML_W, ML_W]
    return np.concatenate([[0], np.cumsum(sizes)]).tolist()


def kernel(x_prompt, x_sample, c_prompt, c_sample, ada_w, ada_b, norm1_g, norm2_g, w_in, w_out, qk_norm_g,
           diff_lambda, attn_out_g, hy_conv_w, hy_conv_b, hy_f_w1, hy_f_b1, hy_f_freq, hy_f_w2, hy_f_b2,
           hy_f_w3, hy_f_b3, hy_bias, hy_out_g, ml_gate_b, ml_out_g, router_w, router_b, exp_w_gate,
           exp_w_up, exp_w_down, sh_w_gate, sh_w_up, sh_w_down):
    nb1, s1, d = x_prompt.shape
    nb2, s2, _ = x_sample.shape
    lay = _Layout(nb1, s1, nb2, s2)
    t = lay.t
    offs = _col_offsets()
    n_main = offs[-1]
    q_off, k_off, v_off, hy_off, mq_off, mk_off, mv_off, mo_off = offs[:8]
    assert HY_W // HY_GROUPS == LANES and ATT_V == LANES

    x = jnp.concatenate([x_prompt.reshape(nb1 * s1, d), x_sample.reshape(nb2 * s2, d)], axis=0)
    c_all = jnp.concatenate([c_prompt, c_sample], axis=0).astype(f32)
    c_pad = jnp.pad(c_all, ((0, (-lay.nseq) % 8), (0, 0)))
    mod = ada_modulation(c_pad, ada_w, ada_b, tn=min(1024, d))
    mod = mod.reshape(DEPTH, c_pad.shape[0], N_MOD, d)

    w_in_b = w_in.astype(bf16)
    w_gate_b = jnp.pad(w_in[:, :, n_main:], ((0, 0), (0, 0), (0, LANES - (w_in.shape[2] - n_main)))).astype(bf16)
    w_out_b = w_out.astype(bf16)
    wg_b, wu_b, wd_b = exp_w_gate.astype(bf16), exp_w_up.astype(bf16), exp_w_down.astype(bf16)
    swg_b, swu_b, swd_b = (sh_w_gate.astype(bf16)[:, None], sh_w_up.astype(bf16)[:, None],
                           sh_w_down.astype(bf16)[:, None])
    dft = {sl: dft_matrices(sl) for sl in sorted({s1, s2})}
    slopes = jnp.asarray(2.0 ** (-8.0 * np.arange(1, ATT_HEADS + 1) / ATT_HEADS), f32)

    tm_mm = math.gcd(TM_MM, s2)
    for l in range(DEPTH):
        lam_init = 0.8 - 0.6 * math.exp(-0.3 * l)
        h1 = norm_modulate(x, mod, norm1_g, l, lay, 0, 1)[0]
        proj = matmul(h1, w_in_b, l, n_main, tm_mm, TN_IN)
        gates = matmul(h1, w_gate_b, l, LANES, tm_mm, LANES)

        lp = diff_lambda[l].astype(f32)
        lam = jnp.exp(jnp.sum(lp[0] * lp[1])) - jnp.exp(jnp.sum(lp[2] * lp[3])) + lam_init
        scal = jnp.concatenate([slopes, lam[None], jnp.zeros((3,), f32)])
        qa, qb, kn, vb = attn_prepare(proj, qk_norm_g[l].reshape(2, ATT_V).astype(f32), tm_mm)
        out_g = attn_out_g[l].reshape(1, ATT_V).astype(f32)
        ya = jnp.concatenate([
            attention_group(scal, qa, qb, kn, vb, out_g, rs, sl, ns, TQ_ATT, TK_ATT, 1.0 - lam_init)
            for rs, sl, ns in lay.groups()], axis=0)

        u, ub = hyena_short_conv(proj, hy_off, hy_conv_w, hy_conv_b, l, lay, TM_ROW, math.gcd(hy_off, HY_W, 512))
        yb_parts = []
        for rs, sl, ns in lay.groups():
            cos_m, sin_m, sin_t = dft[sl]
            p, q = hyena_time_filters(sl, hy_f_w1[l], hy_f_b1[l], hy_f_freq[l], hy_f_w2[l], hy_f_b2[l],
                                      hy_f_w3[l], hy_f_b3[l])
            hr, hi = hyena_filter_spectrum(cos_m, sin_m, p, q, min(TM_DFT, sl), TN_DFT)
            yb_parts.append(hyena_group(u, ub, (cos_m, sin_m, sin_t, hr, hi), hy_bias, hy_out_g, l, rs, sl, ns))
        yb = jnp.concatenate(yb_parts, axis=0)

        gcol, grow = mlstm_gate_tables(gates, ml_gate_b[l], ML_CHUNK)
        hf = mlstm_scan(proj, gcol, grow, lay, ML_CHUNK, False, mq_off, mk_off, mv_off)
        hb = mlstm_scan(proj, gcol, grow, lay, ML_CHUNK, True, mq_off, mk_off, mv_off)
        yc = mlstm_post(hf, hb, proj, mo_off, ml_out_g, l, tm_mm)

        ycat = jnp.concatenate([ya, yb, yc], axis=1)
        x = matmul_gated_residual(ycat, w_out_b, l, x, mod, lay, 2, tm_mm, TN_OUT)

        h2, logits = norm_modulate(x, mod, norm2_g, l, lay, 3, 4, router_w)
        idx, gate = route_tokens(logits, router_b[l])
        row_tok, pos, block_e, n_used = dispatch_plan(idx, TB_MOE)
        x_rows = jnp.take(h2, row_tok, axis=0)
        y_rows = grouped_ffn(x_rows, block_e, n_used, wg_b, wu_b, wd_b, l, TB_MOE, TF_MOE, bf16)
        moe_y = jnp.sum(jnp.take(y_rows, pos, axis=0).astype(f32) * gate[:, :, None], axis=1)
        tb_sh = math.gcd(TB_MOE, t)
        shared_y = grouped_ffn(h2, jnp.zeros((t // tb_sh,), jnp.int32), jnp.full((1,), t // tb_sh, jnp.int32),
                               swg_b, swu_b, swd_b, l, tb_sh, TF_MOE, bf16)
        x = gated_residual_sum(x, moe_y, shared_y, mod, l, lay, 5, TM_ROW)

    return (x[:lay.p].reshape(nb1, s1, d), x[lay.p:].reshape(nb2, s2, d))
```

```python
import functools
import math

import jax
import jax.numpy as jnp
import numpy as np
from jax import lax
from jax.experimental import pallas as pl
from jax.experimental.pallas import tpu as pltpu

DEPTH = 2
ATT_QK = 64
ATT_V = 2 * ATT_QK
ATT_HEADS = 12
ATT_W = ATT_HEADS * ATT_V
HY_W = 1024
HY_ORDER = 2
HY_GROUPS = 8
HY_EMB = 33
HY_TARGET = 1e-2
HY_FAST = 0.3
HY_SLOW = 1.5
ML_V = 256
ML_QK = ML_V // 2
ML_HEADS = 6
ML_W = ML_HEADS * ML_V
N_EXPERTS = 64
TOP_K = 8
N_GROUPS = 8
TOPK_GROUPS = 4
ROUTED_SCALE = 2.5
N_MOD = 6
EPS = 1e-6

LANES = 128
VMEM_LIMIT = 56 * 1024 * 1024

ML_CHUNK = 256
TB_MOE = 512
TF_MOE = 256
TN_MOE = 2048
TM_MM = 1024
TN_IN = 768
TN_OUT = 512
TM_ROW = 256
TQ_ATT = 512
TK_ATT = 512
TW_ATT = 2048
TM_DFT = 512
TN_DFT = 256

f32 = jnp.float32
bf16 = jnp.bfloat16
HI = lax.Precision.HIGHEST


def _cparams(*sem):
    return pltpu.CompilerParams(dimension_semantics=sem, vmem_limit_bytes=VMEM_LIMIT)


class _Layout:
    def __init__(self, nb1, s1, nb2, s2):
        self.nb1, self.s1, self.nb2, self.s2 = nb1, s1, nb2, s2
        self.p = nb1 * s1
        self.t = self.p + nb2 * s2
        self.nseq = nb1 + nb2

    def seq_of_row(self, r):
        return jnp.where(r < self.p, r // self.s1, self.nb1 + (r - self.p) // self.s2)

    def is_boundary(self, r):
        return jnp.where(r <= self.p, r % self.s1 == 0, (r - self.p) % self.s2 == 0)

    def groups(self):
        return ((0, self.s1, self.nb1), (self.p, self.s2, self.nb2))


def _ada_body(c_ref, w_ref, b_ref, o_ref):
    c = c_ref[...]
    s = c * jax.nn.sigmoid(c)
    o_ref[...] = jnp.dot(s, w_ref[...], preferred_element_type=f32, precision=HI) + b_ref[...]


def ada_modulation(c_pad, ada_w, ada_b, tn):
    r, d = c_pad.shape
    n = ada_w.shape[2]
    return pl.pallas_call(
        _ada_body,
        grid=(DEPTH, n // tn),
        in_specs=[pl.BlockSpec((r, d), lambda l, j: (0, 0)),
                  pl.BlockSpec((None, d, tn), lambda l, j: (l, 0, j)),
                  pl.BlockSpec((None, 1, tn), lambda l, j: (l, 0, j))],
        out_specs=pl.BlockSpec((None, r, tn), lambda l, j: (l, 0, j)),
        out_shape=jax.ShapeDtypeStruct((DEPTH, r, n), f32),
        compiler_params=_cparams("parallel", "parallel"),
        name="ada_modulation",
    )(c_pad, ada_w, ada_b.reshape(DEPTH, 1, n))


def _norm_mod_body(x_ref, mod_ref, g_ref, *rest, shift_idx, scale_idx, with_router):
    x = x_ref[...]
    ms = jnp.mean(x * x, axis=-1, keepdims=True)
    y = x * lax.rsqrt(ms + EPS) * g_ref[...]
    h = y * (1.0 + mod_ref[scale_idx:scale_idx + 1, :]) + mod_ref[shift_idx:shift_idx + 1, :]
    if with_router:
        rw_ref, h_ref, logit_ref = rest
        logit_ref[...] = jnp.dot(h, rw_ref[...], preferred_element_type=f32, precision=HI)
    else:
        (h_ref,) = rest
    h_ref[...] = h.astype(bf16)


def norm_modulate(x, mod, gain, l, lay, shift_idx, scale_idx, router_w=None):
    t, d = x.shape
    tm = TM_ROW
    seq = lambda i: lay.seq_of_row(i * tm)
    in_specs = [pl.BlockSpec((tm, d), lambda i: (i, 0)),
                pl.BlockSpec((None, None, N_MOD, d), lambda i: (l, seq(i), 0, 0)),
                pl.BlockSpec((None, 1, d), lambda i: (l, 0, 0))]
    args = [x, mod, gain.reshape(DEPTH, 1, d)]
    out_specs = [pl.BlockSpec((tm, d), lambda i: (i, 0))]
    out_shape = [jax.ShapeDtypeStruct((t, d), bf16)]
    if router_w is not None:
        e = router_w.shape[2]
        in_specs.append(pl.BlockSpec((None, d, e), lambda i: (l, 0, 0)))
        args.append(router_w)
        out_specs.append(pl.BlockSpec((tm, e), lambda i: (i, 0)))
        out_shape.append(jax.ShapeDtypeStruct((t, e), f32))
    return pl.pallas_call(
        functools.partial(_norm_mod_body, shift_idx=shift_idx, scale_idx=scale_idx,
                          with_router=router_w is not None),
        grid=(t // tm,), in_specs=in_specs, out_specs=out_specs, out_shape=out_shape,
        compiler_params=_cparams("parallel"),
        name="norm_modulate",
    )(*args)


def _mm_body(a_ref, b_ref, o_ref):
    o_ref[...] = jnp.dot(a_ref[...], b_ref[...], preferred_element_type=f32).astype(o_ref.dtype)


def matmul(a, b, l, n_out, tm, tn, out_dtype=f32):
    m, k = a.shape
    return pl.pallas_call(
        _mm_body,
        grid=(m // tm, n_out // tn),
        in_specs=[pl.BlockSpec((tm, k), lambda i, j: (i, 0)),
                  pl.BlockSpec((None, k, tn), lambda i, j: (l, 0, j))],
        out_specs=pl.BlockSpec((tm, tn), lambda i, j: (i, j)),
        out_shape=jax.ShapeDtypeStruct((m, n_out), out_dtype),
        compiler_params=_cparams("parallel", "parallel"),
        name="matmul",
    )(a, b)


def _mm_res_body(a_ref, b_ref, res_ref, mod_ref, o_ref, *, gate_idx):
    acc = jnp.dot(a_ref[...], b_ref[...], preferred_element_type=f32)
    o_ref[...] = res_ref[...] + mod_ref[gate_idx:gate_idx + 1, :] * acc


def matmul_gated_residual(a, b, l, res, mod, lay, gate_idx, tm, tn):
    m, k = a.shape
    n = res.shape[1]
    seq = lambda i: lay.seq_of_row(i * tm)
    return pl.pallas_call(
        functools.partial(_mm_res_body, gate_idx=gate_idx),
        grid=(m // tm, n // tn),
        in_specs=[pl.BlockSpec((tm, k), lambda i, j: (i, 0)),
                  pl.BlockSpec((None, k, tn), lambda i, j: (l, 0, j)),
                  pl.BlockSpec((tm, tn), lambda i, j: (i, j)),
                  pl.BlockSpec((None, None, N_MOD, tn), lambda i, j: (l, seq(i), 0, j))],
        out_specs=pl.BlockSpec((tm, tn), lambda i, j: (i, j)),
        out_shape=jax.ShapeDtypeStruct((m, n), f32),
        compiler_params=_cparams("parallel", "parallel"),
        name="matmul_gated_residual",
    )(a, b, res, mod)


POS_BLOCK = 1024
POS_LANE0 = ATT_QK
LOG2E = math.log2(math.e)


def _attn_pre_body(q_ref, k_ref, v_ref, g_ref, qn_ref, kn_ref, vb_ref, *, tm, lay):
    lane = lax.broadcasted_iota(jnp.int32, q_ref.shape, 1)
    lo = lane < ATT_QK

    def half_norm(x, g):
        sq = x * x
        s_lo = jnp.sum(jnp.where(lo, sq, 0.0), axis=-1, keepdims=True)
        s_hi = jnp.sum(jnp.where(lo, 0.0, sq), axis=-1, keepdims=True)
        inv = jnp.where(lo, lax.rsqrt(s_lo / ATT_QK + EPS), lax.rsqrt(s_hi / ATT_QK + EPS))
        return x * inv * g

    qn = half_norm(q_ref[...], g_ref[0:1, :]) * (ATT_QK ** -0.5 * LOG2E)
    kn = half_norm(k_ref[...], g_ref[1:2, :])
    row = pl.program_id(0) * tm + lax.broadcasted_iota(jnp.int32, q_ref.shape, 0)
    r = jnp.where(row < lay.p, row % lay.s1, (row - lay.p) % lay.s2) % POS_BLOCK
    hi = ((r // LANES) * LANES).astype(f32)
    lo_part = (r % LANES).astype(f32)
    pos = jnp.where((lane >= POS_LANE0) & (lane < POS_LANE0 + 3), hi,
                    jnp.where((lane >= POS_LANE0 + 3) & (lane < POS_LANE0 + 6), lo_part, 0.0))
    for mi in range(2):
        qm = qn if mi == 0 else pltpu.roll(qn, ATT_QK, axis=1)
        km = kn if mi == 0 else pltpu.roll(kn, ATT_QK, axis=1)
        qn_ref[mi] = jnp.where(lo, qm, 0.0).astype(bf16)
        kn_ref[mi] = jnp.where(lo, km, pos).astype(bf16)
    vb_ref[...] = v_ref[...].astype(bf16)


def attn_prepare(proj, qk_g2, lay, tm):
    t = proj.shape[0]
    hd = ATT_HEADS
    blk = lambda off: pl.BlockSpec((tm, ATT_V), lambda i, h: (i, off + h))
    qk_spec = pl.BlockSpec((None, 2, tm, ATT_V), lambda i, h: (h, 0, i, 0))
    qk_shp = jax.ShapeDtypeStruct((hd, 2, t, ATT_V), bf16)
    return pl.pallas_call(
        functools.partial(_attn_pre_body, tm=tm, lay=lay),
        grid=(t // tm, hd),
        in_specs=[blk(0), blk(hd), blk(2 * hd), pl.BlockSpec((2, ATT_V), lambda i, h: (0, 0))],
        out_specs=[qk_spec, qk_spec, pl.BlockSpec((None, tm, ATT_V), lambda i, h: (h, i, 0))],
        out_shape=[qk_shp, qk_shp, jax.ShapeDtypeStruct((hd, t, ATT_V), bf16)],
        compiler_params=_cparams("parallel", "parallel"),
        name="attn_prepare",
    )(proj, proj, proj, qk_g2)


def alibi_slope_lanes():
    slopes = 2.0 ** (-8.0 * np.arange(1, ATT_HEADS + 1) / ATT_HEADS) * LOG2E
    c = jnp.asarray(slopes, f32)
    c1 = c.astype(bf16)
    c2 = (c - c1.astype(f32)).astype(bf16)
    c3 = (c - c1.astype(f32) - c2.astype(f32)).astype(bf16)
    pieces = jnp.stack([c1, c2, c3, c1, c2, c3], axis=1)
    lanes = jnp.zeros((ATT_HEADS, 1, ATT_V), bf16).at[:, 0, POS_LANE0:POS_LANE0 + 6].set(pieces)
    return c, lanes


def _attn_body(scal_ref, q_ref, k_ref, v_ref, cl_ref, g_ref, o_ref, m_ref, l_ref, acc_ref, s_ref,
               *, tq, tk, tw, seq_len, out_scale):
    h, qi = pl.program_id(0), pl.program_id(1)
    c = scal_ref[h]
    qpos0 = (qi * tq) % seq_len
    ipos = (qpos0 + lax.broadcasted_iota(jnp.int32, (tq, 1), 0)).astype(f32)
    lane = lax.broadcasted_iota(jnp.int32, (tq, ATT_V), 1)
    pos_lane = (lane >= POS_LANE0) & (lane < POS_LANE0 + 6)
    nt = (((1,), (1,)), ((), ()))
    m_ref[...] = jnp.full(m_ref.shape, -jnp.inf, f32)
    l_ref[...] = jnp.zeros(l_ref.shape, f32)
    acc_ref[...] = jnp.zeros(acc_ref.shape, f32)
    cl = cl_ref[...]

    def wide_tile(wb, carry):
        w0 = pl.multiple_of(wb * tw, tw)
        for sub in range(tw // tk):
            j0 = w0 + sub * tk
            cols = slice(sub * tk, (sub + 1) * tk)
            is_left = j0 + tk <= qpos0
            is_right = j0 >= qpos0 + tq
            sgn = jnp.where(is_left, 1.0, jnp.where(is_right, -1.0, 0.0))
            off = (sgn * c) * (((j0 // POS_BLOCK) * POS_BLOCK).astype(f32) - ipos)
            for mi in range(2):
                qv = jnp.where(pos_lane, sgn.astype(cl.dtype) * cl, q_ref[mi])
                s_ref[mi, :, cols] = lax.dot_general(qv, k_ref[mi, pl.ds(j0, tk), :], nt,
                                                     preferred_element_type=f32) + off

            @pl.when(jnp.logical_not(is_left | is_right))
            def _():
                rc = lax.broadcasted_iota(jnp.int32, (tq, tk), 0) - lax.broadcasted_iota(jnp.int32, (tq, tk), 1)
                bias = c * jnp.abs(rc + (qpos0 - j0)).astype(f32)
                for mi in range(2):
                    s_ref[mi, :, cols] = s_ref[mi, :, cols] - bias

        vt = v_ref[pl.ds(w0, tw), :]
        for mi in range(2):
            s = s_ref[mi]
            m_prev = m_ref[mi]
            m_new = jnp.maximum(m_prev, jnp.max(s, axis=-1, keepdims=True))
            alpha = jnp.exp2(m_prev - m_new)
            p = jnp.exp2(s - m_new)
            l_ref[mi] = alpha * l_ref[mi] + jnp.sum(p, axis=-1, keepdims=True)
            acc_ref[mi] = alpha * acc_ref[mi] + jnp.dot(p.astype(bf16), vt, preferred_element_type=f32)
            m_ref[mi] = m_new
        return carry

    lax.fori_loop(0, seq_len // tw, wide_tile, 0)

    lam = scal_ref[ATT_HEADS]
    o = acc_ref[0] / l_ref[0] - lam * (acc_ref[1] / l_ref[1])
    ms = jnp.mean(o * o, axis=-1, keepdims=True)
    o_ref[...] = (o * lax.rsqrt(ms + EPS) * g_ref[...] * out_scale).astype(o_ref.dtype)


def attention_group(scal, c_lanes, qn, kn, vb, out_g, row_start, seq_len, nseq, tq, tk, tw, out_scale):
    tq, tk, tw = min(tq, seq_len), min(tk, seq_len), min(tw, seq_len)
    assert POS_BLOCK % tk == 0 and tw % tk == 0
    q0, s0 = row_start // tq, row_start // seq_len
    rows = nseq * seq_len
    return pl.pallas_call(
        functools.partial(_attn_body, tq=tq, tk=tk, tw=tw, seq_len=seq_len, out_scale=out_scale),
        grid=(ATT_HEADS, rows // tq),
        in_specs=[pl.BlockSpec(memory_space=pltpu.SMEM),
                  pl.BlockSpec((None, 2, tq, ATT_V), lambda h, qi: (h, 0, q0 + qi, 0)),
                  pl.BlockSpec((None, 2, seq_len, ATT_V), lambda h, qi: (h, 0, s0 + (qi * tq) // seq_len, 0)),
                  pl.BlockSpec((None, seq_len, ATT_V), lambda h, qi: (h, s0 + (qi * tq) // seq_len, 0)),
                  pl.BlockSpec((None, 1, ATT_V), lambda h, qi: (h, 0, 0)),
                  pl.BlockSpec((1, ATT_V), lambda h, qi: (0, 0))],
        out_specs=pl.BlockSpec((tq, ATT_V), lambda h, qi: (qi, h)),
        out_shape=jax.ShapeDtypeStruct((rows, ATT_W), bf16),
        scratch_shapes=[pltpu.VMEM((2, tq, 1), f32), pltpu.VMEM((2, tq, 1), f32),
                        pltpu.VMEM((2, tq, ATT_V), f32), pltpu.VMEM((2, tq, tw), f32)],
        compiler_params=_cparams("parallel", "parallel"),
        name="diff_attention",
    )(scal, qn, kn, vb, c_lanes, out_g)


def _mlstm_body(q_ref, k_ref, v_ref, gc_ref, gr_ref, h_ref, c_ref, n_ref, m_ref, *, lc, lay, backward):
    c = pl.program_id(0)
    nchunks = pl.num_programs(0)
    heads = ML_HEADS
    ceff = nchunks - 1 - c if backward else c
    r0 = ceff * lc
    reset = lay.is_boundary(r0 + lc) if backward else lay.is_boundary(r0)

    @pl.when(reset)
    def _():
        c_ref[...] = jnp.zeros(c_ref.shape, f32)
        n_ref[...] = jnp.zeros(n_ref.shape, f32)
        m_ref[...] = jnp.zeros(m_ref.shape, f32)

    row = lax.broadcasted_iota(jnp.int32, (lc, lc), 0)
    col = lax.broadcasted_iota(jnp.int32, (lc, lc), 1)
    mask = (col >= row) if backward else (col <= row)
    gcol = gc_ref[...]
    lane = lax.broadcasted_iota(jnp.int32, gcol.shape, 1)
    d = 1 if backward else 0

    def col_of(idx):
        return jnp.sum(jnp.where(lane == idx, gcol, 0.0), axis=-1, keepdims=True)

    for h in range(heads):
        ia = d * heads + h
        a_col, i_col = col_of(ia), col_of(2 * heads + ia)
        a_row = gr_ref[ia:ia + 1, :]
        i_row = gr_ref[2 * heads + ia:2 * heads + ia + 1, :]
        g11 = gr_ref[4 * heads + ia:4 * heads + ia + 1, 0:1]
        m11 = m_ref[h][:, 0:1]
        qf = q_ref[:, h * ML_QK:(h + 1) * ML_QK] * (ML_QK ** -0.5)
        kf = k_ref[:, h * ML_QK:(h + 1) * ML_QK]
        qb, kb = qf.astype(bf16), kf.astype(bf16)
        vb = v_ref[:, h * ML_V:(h + 1) * ML_V].astype(bf16)
        dmat = jnp.where(mask, a_col - a_row + i_row, -jnp.inf)
        m_inter = a_col + m11
        mj = jnp.maximum(jnp.max(dmat, axis=-1, keepdims=True), m_inter)
        p = jnp.exp(dmat - mj)
        s = lax.dot_general(qb, kb, (((1,), (1,)), ((), ())), preferred_element_type=f32)
        wts = p * s
        inter = jnp.exp(m_inter - mj)
        cmat = c_ref[h]
        num = (jnp.dot(wts.astype(bf16), vb, preferred_element_type=f32)
               + inter * jnp.dot(qb, cmat.astype(bf16), preferred_element_type=f32))
        nvec = n_ref[h]
        den = jnp.sum(wts, axis=-1, keepdims=True) + inter * jnp.sum(qf * nvec, axis=-1, keepdims=True)
        h_ref[:, h * ML_V:(h + 1) * ML_V] = num / jnp.maximum(jnp.abs(den), jnp.exp(-mj))
        ws = g11 - a_col + i_col
        m_new = jnp.maximum(g11 + m11, jnp.max(ws, axis=0, keepdims=True))
        dec = jnp.exp(g11 + m11 - m_new)
        ek = jnp.exp(ws - m_new) * kf
        c_ref[h] = dec * cmat + lax.dot_general(ek.astype(bf16), vb, (((0,), (0,)), ((), ())),
                                                preferred_element_type=f32)
        n_ref[h] = dec * nvec + jnp.sum(ek, axis=0, keepdims=True)
        m_ref[h] = jnp.broadcast_to(m_new, (1, LANES))


def mlstm_scan(proj, gcol, grow, lay, lc, backward, q_off, k_off, v_off):
    t = proj.shape[0]
    nchunks = t // lc
    hq, hv = ML_HEADS * ML_QK, ML_W
    cidx = (lambda c: nchunks - 1 - c) if backward else (lambda c: c)
    return pl.pallas_call(
        functools.partial(_mlstm_body, lc=lc, lay=lay, backward=backward),
        grid=(nchunks,),
        in_specs=[pl.BlockSpec((lc, hq), lambda c: (cidx(c), q_off // hq)),
                  pl.BlockSpec((lc, hq), lambda c: (cidx(c), k_off // hq)),
                  pl.BlockSpec((lc, hv), lambda c: (cidx(c), v_off // hv)),
                  pl.BlockSpec((lc, LANES), lambda c: (cidx(c), 0)),
                  pl.BlockSpec((grow.shape[0], lc), lambda c: (0, cidx(c)))],
        out_specs=pl.BlockSpec((lc, hv), lambda c: (cidx(c), 0)),
        out_shape=jax.ShapeDtypeStruct((t, hv), f32),
        scratch_shapes=[pltpu.VMEM((ML_HEADS, ML_QK, ML_V), f32), pltpu.VMEM((ML_HEADS, 1, ML_QK), f32),
                        pltpu.VMEM((ML_HEADS, 1, LANES), f32)],
        compiler_params=_cparams("arbitrary"),
        name="mlstm_bwd" if backward else "mlstm_fwd",
    )(proj, proj, proj, gcol, grow)


def _mlstm_post_body(hf_ref, hb_ref, o_ref, g_ref, y_ref):
    hsum = hf_ref[...] + hb_ref[...]
    ms = jnp.mean(hsum * hsum, axis=-1, keepdims=True)
    y_ref[...] = (jax.nn.sigmoid(o_ref[...]) * (hsum * lax.rsqrt(ms + EPS) * g_ref[...])).astype(y_ref.dtype)


def mlstm_post(hf, hb, proj, o_off, ml_g, l, tm):
    t = hf.shape[0]
    blk = lambda off: pl.BlockSpec((tm, ML_V), lambda i, h: (i, off + h))
    return pl.pallas_call(
        _mlstm_post_body,
        grid=(t // tm, ML_HEADS),
        in_specs=[blk(0), blk(0), blk(o_off // ML_V), pl.BlockSpec((None, 1, ML_V), lambda i, h: (l, 0, h))],
        out_specs=blk(0),
        out_shape=jax.ShapeDtypeStruct((t, ML_W), bf16),
        compiler_params=_cparams("parallel", "parallel"),
        name="mlstm_post",
    )(hf, hb, proj, ml_g.reshape(DEPTH, 1, ML_W))


def mlstm_gate_tables(gates, gate_b, lc):
    t = gates.shape[0]
    hd = ML_HEADS
    g = gates[:, :4 * hd].reshape(t, 4, hd) + gate_b.astype(f32)
    i_f, f_f, i_b, f_b = g[:, 0], g[:, 1], g[:, 2], g[:, 3]
    lf_f = jax.nn.log_sigmoid(f_f).reshape(t // lc, lc, hd)
    lf_b = jax.nn.log_sigmoid(f_b).reshape(t // lc, lc, hd)
    a_f = jnp.cumsum(lf_f, axis=1)
    a_b = jnp.flip(jnp.cumsum(jnp.flip(lf_b, axis=1), axis=1), axis=1)
    tot_f = jnp.broadcast_to(a_f[:, -1:, :], a_f.shape)
    tot_b = jnp.broadcast_to(a_b[:, :1, :], a_b.shape)
    flat = lambda a: a.reshape(t, hd)
    cols = jnp.concatenate([flat(a_f), flat(a_b), i_f, i_b, flat(tot_f), flat(tot_b)], axis=1)
    gcol = jnp.pad(cols, ((0, 0), (0, LANES - cols.shape[1])))
    grow = jnp.pad(cols.T, ((0, (-cols.shape[1]) % 8), (0, 0)))
    return gcol, grow


def _hyconv_body(c_ref, p_ref, n_ref, w_ref, b_ref, u_ref, ub_ref, *, tm, lay):
    r0 = pl.program_id(0) * tm
    x = c_ref[...]
    prev_row = jnp.where(lay.is_boundary(r0), 0.0, p_ref[7:8, :])
    next_row = jnp.where(lay.is_boundary(r0 + tm), 0.0, n_ref[0:1, :])
    rid = lax.broadcasted_iota(jnp.int32, x.shape, 0)
    xm = jnp.where(rid == 0, prev_row, pltpu.roll(x, 1, axis=0))
    xp = jnp.where(rid == tm - 1, next_row, pltpu.roll(x, tm - 1, axis=0))
    u = xm * w_ref[0:1, :] + x * w_ref[1:2, :] + xp * w_ref[2:3, :] + b_ref[...]
    u_ref[...] = u
    ub_ref[...] = u.astype(bf16)


def hyena_short_conv(proj, hy_off, conv_w, conv_b, l, lay, tm, tc):
    t = proj.shape[0]
    width = 3 * HY_W
    c0 = hy_off // tc
    nblk8 = t // 8
    return pl.pallas_call(
        functools.partial(_hyconv_body, tm=tm, lay=lay),
        grid=(t // tm, width // tc),
        in_specs=[pl.BlockSpec((tm, tc), lambda i, j: (i, c0 + j)),
                  pl.BlockSpec((8, tc), lambda i, j: (jnp.maximum(i * (tm // 8) - 1, 0), c0 + j)),
                  pl.BlockSpec((8, tc), lambda i, j: (jnp.minimum((i + 1) * (tm // 8), nblk8 - 1), c0 + j)),
                  pl.BlockSpec((None, 3, tc), lambda i, j: (l, 0, j)),
                  pl.BlockSpec((None, 1, tc), lambda i, j: (l, 0, j))],
        out_specs=[pl.BlockSpec((tm, tc), lambda i, j: (i, j))] * 2,
        out_shape=[jax.ShapeDtypeStruct((t, width), f32), jax.ShapeDtypeStruct((t, width), bf16)],
        compiler_params=_cparams("parallel", "parallel"),
        name="hyena_short_conv",
    )(proj, proj, proj, conv_w, conv_b.reshape(DEPTH, 1, width))


def hyena_time_filters(seq_len, w1, b1, freq, w2, b2, w3, b3):
    t = jnp.linspace(0.0, 1.0, seq_len, dtype=f32)[:, None]
    bands = (HY_EMB - 1) // 2
    fr = jnp.linspace(1e-4, bands - 1, bands, dtype=f32)
    ang = (2.0 * math.pi / seq_len) * jnp.arange(seq_len, dtype=f32)[:, None] * fr[None, :]
    z = jnp.concatenate([t, jnp.cos(ang), -jnp.sin(ang)], axis=-1)
    fq = freq.astype(f32)
    h = jnp.sin(fq[0] * (jnp.dot(z, w1, precision=HI) + b1))
    h = jnp.sin(fq[1] * (jnp.dot(h, w2, precision=HI) + b2))
    h = (jnp.dot(h, w3, precision=HI) + b3).reshape(seq_len, HY_ORDER, 2, HY_W)
    deltas = jnp.linspace(math.log(HY_TARGET) / HY_SLOW, math.log(HY_TARGET) / HY_FAST, HY_W, dtype=f32)
    h = h * jnp.exp(-t * jnp.abs(deltas))[:, None, None, :]
    hf, hb = h[:, :, 0], h[:, :, 1]
    first = (jnp.arange(seq_len) == 0)[:, None, None]
    hf = jnp.where(first, hf + hb, hf)
    hb = jnp.where(first, 0.0, hb)
    p = (hf + hb).reshape(seq_len, HY_ORDER * HY_W)
    q = (hf - hb).reshape(seq_len, HY_ORDER * HY_W)
    return p.astype(bf16), q.astype(bf16)


def dft_matrices(seq_len):
    idx = jnp.arange(seq_len, dtype=jnp.int32)
    r = (idx[:, None] * idx[None, :]) % (2 * seq_len)
    ang = r.astype(f32) * (math.pi / seq_len)
    cos_m = jnp.cos(ang).astype(bf16)
    neg_sin = lax.optimization_barrier((-jnp.sin(ang)).astype(bf16))
    alt = jnp.where(idx % 2 == 0, 1.0, -1.0).astype(bf16)
    sin_m = jnp.where(idx[:, None] == 0, alt[None, :], neg_sin)
    sin_t = jnp.where(idx[None, :] == 0, alt[:, None], neg_sin)
    return cos_m, sin_m, sin_t


def _hy_filter_body(fc_ref, fs_ref, p_ref, q_ref, hr_ref, hi_ref):
    p = p_ref[...]
    hr_ref[...] = jnp.dot(fc_ref[...], p, preferred_element_type=f32)
    hi = jnp.dot(fs_ref[...], q_ref[...], preferred_element_type=f32)
    nyq = jnp.dot(fs_ref[0:8, :], p, preferred_element_type=f32)[0:1, :]
    grow = pl.program_id(0) * hr_ref.shape[0] + lax.broadcasted_iota(jnp.int32, hi.shape, 0)
    hi_ref[...] = jnp.where(grow == 0, nyq, hi)


def hyena_filter_spectrum(cos_m, sin_m, p, q, tm, tn):
    seq_len, n = p.shape
    a_spec = pl.BlockSpec((tm, seq_len), lambda i, j: (i, 0))
    b_spec = pl.BlockSpec((seq_len, tn), lambda i, j: (0, j))
    o_spec = pl.BlockSpec((tm, tn), lambda i, j: (i, j))
    shp = jax.ShapeDtypeStruct((seq_len, n), f32)
    return pl.pallas_call(
        _hy_filter_body,
        grid=(seq_len // tm, n // tn),
        in_specs=[a_spec, a_spec, b_spec, b_spec], out_specs=[o_spec, o_spec], out_shape=[shp, shp],
        compiler_params=_cparams("parallel", "parallel"),
        name="hyena_filter_spectrum",
    )(cos_m, sin_m, p, q)


def _hy_fwd_body(fc_ref, fs_ref, z_ref, hr_ref, hi_ref, yr_ref, yi_ref, *, n_fft):
    z = z_ref[...]
    vr = jnp.dot(fc_ref[...], z, preferred_element_type=f32)
    vi = jnp.dot(fs_ref[...], z, preferred_element_type=f32)
    hr, hi = hr_ref[...], hi_ref[...]
    grow = pl.program_id(0) * vr.shape[0] + lax.broadcasted_iota(jnp.int32, vr.shape, 0)
    row0 = grow == 0
    vihi = vi * hi
    yr = vr * hr - jnp.where(row0, 0.0, vihi)
    yi = jnp.where(row0, vihi, vr * hi + vi * hr)
    scale = jnp.where(row0, 1.0 / n_fft, 2.0 / n_fft)
    yr_ref[...] = (yr * scale).astype(bf16)
    yi_ref[...] = (yi * scale).astype(bf16)


def hyena_fwd(cos_m, sin_m, zb, z_col, hr, hi, order, row_start, seq_len, nseq, tm, tn):
    tm, tn = min(tm, seq_len), min(tn, HY_W)
    rows = nseq * seq_len
    s0, i0 = row_start // seq_len, seq_len // tm
    a_spec = pl.BlockSpec((tm, seq_len), lambda i, s, j: (i, 0))
    h_spec = pl.BlockSpec((tm, tn), lambda i, s, j: (i, order * (HY_W // tn) + j))
    o_spec = pl.BlockSpec((tm, tn), lambda i, s, j: (s * i0 + i, j))
    shp = jax.ShapeDtypeStruct((rows, HY_W), bf16)
    return pl.pallas_call(
        functools.partial(_hy_fwd_body, n_fft=2 * seq_len),
        grid=(seq_len // tm, nseq, HY_W // tn),
        in_specs=[a_spec, a_spec,
                  pl.BlockSpec((seq_len, tn), lambda i, s, j: (s0 + s, z_col // tn + j)),
                  h_spec, h_spec],
        out_specs=[o_spec, o_spec], out_shape=[shp, shp],
        compiler_params=_cparams("parallel", "parallel", "parallel"),
        name="hyena_fwd_dft",
    )(cos_m, sin_m, zb, hr, hi)


def _hy_inv_body(fc_ref, ft_ref, yr_ref, yi_ref, z_ref, x_ref, d_ref, *rest, last):
    y = (jnp.dot(fc_ref[...], yr_ref[...], preferred_element_type=f32)
         + jnp.dot(ft_ref[...], yi_ref[...], preferred_element_type=f32))
    z = x_ref[...] * (y + d_ref[...] * z_ref[...])
    if last:
        g_ref, o_ref = rest
        gw = HY_W // HY_GROUPS
        for gi in range(z.shape[1] // gw):
            seg = z[:, gi * gw:(gi + 1) * gw]
            ms = jnp.mean(seg * seg, axis=-1, keepdims=True)
            o_ref[:, gi * gw:(gi + 1) * gw] = (seg * lax.rsqrt(ms + EPS)
                                               * g_ref[:, gi * gw:(gi + 1) * gw]).astype(o_ref.dtype)
    else:
        o_ref, ob_ref = rest
        o_ref[...] = z
        ob_ref[...] = z.astype(bf16)


def hyena_inv(cos_m, sin_t, yr, yi, z_arr, z_row0, z_col, u, x_col, d_bias, gain, l, order,
              row_start, seq_len, nseq, tm, tn):
    tm, tn = min(tm, seq_len), min(tn, HY_W)
    last = order == HY_ORDER - 1
    rows = nseq * seq_len
    i0 = seq_len // tm
    a_spec = pl.BlockSpec((tm, seq_len), lambda i, s, j: (i, 0))
    b_spec = pl.BlockSpec((seq_len, tn), lambda i, s, j: (s, j))
    e_spec = lambda row0, col: pl.BlockSpec((tm, tn), lambda i, s, j: (row0 // tm + s * i0 + i, col // tn + j))
    o_spec = pl.BlockSpec((tm, tn), lambda i, s, j: (s * i0 + i, j))
    in_specs = [a_spec, a_spec, b_spec, b_spec, e_spec(z_row0, z_col), e_spec(row_start, x_col),
                pl.BlockSpec((None, 1, tn), lambda i, s, j: (l * HY_ORDER + order, 0, j))]
    args = [cos_m, sin_t, yr, yi, z_arr, u, d_bias.reshape(DEPTH * HY_ORDER, 1, HY_W)]
    if last:
        in_specs.append(pl.BlockSpec((None, 1, tn), lambda i, s, j: (l, 0, j)))
        args.append(gain.reshape(DEPTH, 1, HY_W))
        out_specs = o_spec
        out_shape = jax.ShapeDtypeStruct((rows, HY_W), bf16)
    else:
        out_specs = [o_spec, o_spec]
        out_shape = [jax.ShapeDtypeStruct((rows, HY_W), f32), jax.ShapeDtypeStruct((rows, HY_W), bf16)]
    return pl.pallas_call(
        functools.partial(_hy_inv_body, last=last),
        grid=(seq_len // tm, nseq, HY_W // tn),
        in_specs=in_specs, out_specs=out_specs, out_shape=out_shape,
        compiler_params=_cparams("parallel", "parallel", "parallel"),
        name="hyena_inv_dft",
    )(*args)


def hyena_group(u, ub, spec, d_bias, gain, l, row_start, seq_len, nseq):
    cos_m, sin_m, sin_t, hr, hi = spec
    grp = (row_start, seq_len, nseq, TM_DFT, TN_DFT)
    yr, yi = hyena_fwd(cos_m, sin_m, ub, 0, hr, hi, 0, *grp)
    z1, z1b = hyena_inv(cos_m, sin_t, yr, yi, u, row_start, 0, u, HY_W, d_bias, gain, l, 0, *grp)
    yr, yi = hyena_fwd(cos_m, sin_m, z1b, 0, hr, hi, 1, 0, seq_len, nseq, TM_DFT, TN_DFT)
    return hyena_inv(cos_m, sin_t, yr, yi, z1, 0, 0, u, 2 * HY_W, d_bias, gain, l, 1, *grp)


def _item_changed(w, e_ref, c_ref):
    prev = jnp.maximum(w - 1, 0)
    return (w == 0) | (e_ref[w] != e_ref[prev]) | (c_ref[w] != c_ref[prev])


def _ffn_up_body(ib_ref, ic_ref, ie_ref, n_ref, x_ref, wg_ref, wu_ref, h_ref, wgb_ref, wub_ref):
    w = pl.program_id(0)

    @pl.when(w < n_ref[0])
    def _():
        @pl.when(_item_changed(w, ie_ref, ic_ref))
        def _():
            wgb_ref[...] = wg_ref[...].astype(bf16)
            wub_ref[...] = wu_ref[...].astype(bf16)

        x = x_ref[...]
        g = jnp.dot(x, wgb_ref[...], preferred_element_type=f32)
        u = jnp.dot(x, wub_ref[...], preferred_element_type=f32)
        h_ref[...] = (g * jax.nn.sigmoid(g) * u).astype(h_ref.dtype)


def _ffn_down_body(ib_ref, ic_ref, ie_ref, n_ref, h_ref, wd_ref, y_ref, wdb_ref):
    w = pl.program_id(0)

    @pl.when(w < n_ref[0])
    def _():
        @pl.when(_item_changed(w, ie_ref, ic_ref))
        def _():
            wdb_ref[...] = wd_ref[...].astype(bf16)

        y_ref[...] = jnp.dot(h_ref[...], wdb_ref[...], preferred_element_type=f32).astype(y_ref.dtype)


def ffn_items(plan, nb, ncol):
    blocks_per_e, first_block_e, n_used = plan
    n_items = n_used[0] * ncol
    w = jnp.minimum(jnp.arange(nb * ncol, dtype=jnp.int32), n_items - 1)
    istart = first_block_e * ncol
    e = (jnp.searchsorted(istart, w, side='right') - 1).astype(jnp.int32)
    local = w - istart[e]
    cnt = jnp.maximum(blocks_per_e[e], 1)
    item_c = (local // cnt).astype(jnp.int32)
    item_b = (first_block_e[e] + local % cnt).astype(jnp.int32)
    return item_b, item_c, e, n_items[None].astype(jnp.int32)


def grouped_ffn(x_rows, plan, wg, wu, wd, l, tb, tf, tn, out_dtype):
    rows, d = x_rows.shape
    fdim = wg.shape[3]
    nb = rows // tb
    h = pl.pallas_call(
        _ffn_up_body,
        grid_spec=pltpu.PrefetchScalarGridSpec(
            num_scalar_prefetch=4, grid=(nb * (fdim // tf),),
            in_specs=[pl.BlockSpec((tb, d), lambda w, ib, ic, ie, n: (ib[w], 0)),
                      pl.BlockSpec((None, None, d, tf), lambda w, ib, ic, ie, n: (l, ie[w], 0, ic[w])),
                      pl.BlockSpec((None, None, d, tf), lambda w, ib, ic, ie, n: (l, ie[w], 0, ic[w]))],
            out_specs=pl.BlockSpec((tb, tf), lambda w, ib, ic, ie, n: (ib[w], ic[w])),
            scratch_shapes=[pltpu.VMEM((d, tf), bf16), pltpu.VMEM((d, tf), bf16)]),
        out_shape=jax.ShapeDtypeStruct((rows, fdim), bf16),
        compiler_params=_cparams("arbitrary"),
        name="ffn_up",
    )(*ffn_items(plan, nb, fdim // tf), x_rows, wg, wu)
    return pl.pallas_call(
        _ffn_down_body,
        grid_spec=pltpu.PrefetchScalarGridSpec(
            num_scalar_prefetch=4, grid=(nb * (d // tn),),
            in_specs=[pl.BlockSpec((tb, fdim), lambda w, ib, ic, ie, n: (ib[w], 0)),
                      pl.BlockSpec((None, None, fdim, tn), lambda w, ib, ic, ie, n: (l, ie[w], 0, ic[w]))],
            out_specs=pl.BlockSpec((tb, tn), lambda w, ib, ic, ie, n: (ib[w], ic[w])),
            scratch_shapes=[pltpu.VMEM((fdim, tn), bf16)]),
        out_shape=jax.ShapeDtypeStruct((rows, d), out_dtype),
        compiler_params=_cparams("arbitrary"),
        name="ffn_down",
    )(*ffn_items(plan, nb, d // tn), h, wd)


def route_tokens(logits, router_b):
    t = logits.shape[0]
    per = N_EXPERTS // N_GROUPS
    s = jax.nn.sigmoid(logits)
    sb = s + router_b.astype(f32)
    gscore = jnp.sum(lax.top_k(sb.reshape(t, N_GROUPS, per), 2)[0], axis=-1)
    _, gidx = lax.top_k(gscore, TOPK_GROUPS)
    gmask = jnp.sum(jax.nn.one_hot(gidx, N_GROUPS, dtype=f32), axis=1) > 0
    emask = jnp.repeat(gmask, per, axis=1)
    _, idx = lax.top_k(jnp.where(emask, sb, -jnp.inf), TOP_K)
    gate = jnp.take_along_axis(s, idx, axis=1)
    gate = gate / jnp.sum(gate, axis=-1, keepdims=True) * ROUTED_SCALE
    return idx, gate


def dispatch_plan(idx, tb):
    t = idx.shape[0]
    n = t * TOP_K
    e_flat = idx.reshape(n).astype(jnp.int32)
    onehot = (e_flat[:, None] == jnp.arange(N_EXPERTS, dtype=jnp.int32)[None, :]).astype(jnp.int32)
    csum = jnp.cumsum(onehot, axis=0)
    counts = csum[-1]
    rank = jnp.take_along_axis(csum, e_flat[:, None], axis=1)[:, 0] - 1
    padded = (counts + tb - 1) // tb * tb
    pends = jnp.cumsum(padded)
    pstarts = pends - padded
    pos = pstarts[e_flat] + rank
    nb = (n + N_EXPERTS * (tb - 1) + tb - 1) // tb
    tok = jnp.arange(n, dtype=jnp.int32) // TOP_K
    row_tok = jnp.zeros((nb * tb,), jnp.int32).at[pos].set(tok, unique_indices=True)
    plan = ((padded // tb).astype(jnp.int32), (pstarts // tb).astype(jnp.int32),
            (pends[-1:] // tb).astype(jnp.int32))
    return row_tok, pos.reshape(t, TOP_K), plan


def _final_body(x_ref, moe_ref, sh_ref, mod_ref, o_ref, *, gate_idx):
    o_ref[...] = x_ref[...] + mod_ref[gate_idx:gate_idx + 1, :] * (moe_ref[...] + sh_ref[...])


def gated_residual_sum(x, moe_y, shared_y, mod, l, lay, gate_idx, tm):
    t, d = x.shape
    seq = lambda i: lay.seq_of_row(i * tm)
    blk = pl.BlockSpec((tm, d), lambda i: (i, 0))
    return pl.pallas_call(
        functools.partial(_final_body, gate_idx=gate_idx),
        grid=(t // tm,),
        in_specs=[blk, blk, blk, pl.BlockSpec((None, None, N_MOD, d), lambda i: (l, seq(i), 0, 0))],
        out_specs=blk, out_shape=jax.ShapeDtypeStruct((t, d), f32),
        compiler_params=_cparams("parallel"),
        name="gated_residual_sum",
    )(x, moe_y, shared_y, mod)


def _col_offsets():
    sizes = [ATT_W, ATT_W, ATT_W, 3 * HY_W, ML_HEADS * ML_QK, ML_HEADS * ML_QK, ML_W, ML_W]
    return np.concatenate([[0], np.cumsum(sizes)]).tolist()


def kernel(x_prompt, x_sample, c_prompt, c_sample, ada_w, ada_b, norm1_g, norm2_g, w_in, w_out, qk_norm_g,
           diff_lambda, attn_out_g, hy_conv_w, hy_conv_b, hy_f_w1, hy_f_b1, hy_f_freq, hy_f_w2, hy_f_b2,
           hy_f_w3, hy_f_b3, hy_bias, hy_out_g, ml_gate_b, ml_out_g, router_w, router_b, exp_w_gate,
           exp_w_up, exp_w_down, sh_w_gate, sh_w_up, sh_w_down):
    nb1, s1, d = x_prompt.shape
    nb2, s2, _ = x_sample.shape
    lay = _Layout(nb1, s1, nb2, s2)
    t = lay.t
    offs = _col_offsets()
    n_main = offs[-1]
    q_off, k_off, v_off, hy_off, mq_off, mk_off, mv_off, mo_off = offs[:8]
    assert HY_W // HY_GROUPS == LANES and ATT_V == LANES

    x = jnp.concatenate([x_prompt.reshape(nb1 * s1, d), x_sample.reshape(nb2 * s2, d)], axis=0)
    c_all = jnp.concatenate([c_prompt, c_sample], axis=0).astype(f32)
    c_pad = jnp.pad(c_all, ((0, (-lay.nseq) % 8), (0, 0)))
    mod = ada_modulation(c_pad, ada_w, ada_b, tn=min(1024, d))
    mod = mod.reshape(DEPTH, c_pad.shape[0], N_MOD, d)

    w_in_b = w_in.astype(bf16)
    w_gate_b = jnp.pad(w_in[:, :, n_main:], ((0, 0), (0, 0), (0, LANES - (w_in.shape[2] - n_main)))).astype(bf16)
    w_out_b = w_out.astype(bf16)
    tf_moe, tn_moe = min(TF_MOE, exp_w_gate.shape[3]), min(TN_MOE, d)
    dft = {sl: dft_matrices(sl) for sl in sorted({s1, s2})}
    slopes, slope_lanes = alibi_slope_lanes()

    tm_mm = math.gcd(TM_MM, s2)
    for l in range(DEPTH):
        lam_init = 0.8 - 0.6 * math.exp(-0.3 * l)
        h1 = norm_modulate(x, mod, norm1_g, l, lay, 0, 1)[0]
        proj = matmul(h1, w_in_b, l, n_main, tm_mm, TN_IN)
        gates = matmul(h1, w_gate_b, l, LANES, tm_mm, LANES)

        lp = diff_lambda[l].astype(f32)
        lam = jnp.exp(jnp.sum(lp[0] * lp[1])) - jnp.exp(jnp.sum(lp[2] * lp[3])) + lam_init
        scal = jnp.concatenate([slopes, lam[None], jnp.zeros((3,), f32)])
        qn, kn, vb = attn_prepare(proj, qk_norm_g[l].reshape(2, ATT_V).astype(f32), lay, tm_mm)
        out_g = attn_out_g[l].reshape(1, ATT_V).astype(f32)
        ya = jnp.concatenate([
            attention_group(scal, slope_lanes, qn, kn, vb, out_g, rs, sl, ns, TQ_ATT, TK_ATT, TW_ATT, 1.0 - lam_init)
            for rs, sl, ns in lay.groups()], axis=0)

        u, ub = hyena_short_conv(proj, hy_off, hy_conv_w, hy_conv_b, l, lay, TM_ROW, math.gcd(hy_off, HY_W, 512))
        yb_parts = []
        for rs, sl, ns in lay.groups():
            cos_m, sin_m, sin_t = dft[sl]
            p, q = hyena_time_filters(sl, hy_f_w1[l], hy_f_b1[l], hy_f_freq[l], hy_f_w2[l], hy_f_b2[l],
                                      hy_f_w3[l], hy_f_b3[l])
            hr, hi = hyena_filter_spectrum(cos_m, sin_m, p, q, min(TM_DFT, sl), TN_DFT)
            yb_parts.append(hyena_group(u, ub, (cos_m, sin_m, sin_t, hr, hi), hy_bias, hy_out_g, l, rs, sl, ns))
        yb = jnp.concatenate(yb_parts, axis=0)

        gcol, grow = mlstm_gate_tables(gates, ml_gate_b[l], ML_CHUNK)
        hf = mlstm_scan(proj, gcol, grow, lay, ML_CHUNK, False, mq_off, mk_off, mv_off)
        hb = mlstm_scan(proj, gcol, grow, lay, ML_CHUNK, True, mq_off, mk_off, mv_off)
        yc = mlstm_post(hf, hb, proj, mo_off, ml_out_g, l, tm_mm)

        ycat = jnp.concatenate([ya, yb, yc], axis=1)
        x = matmul_gated_residual(ycat, w_out_b, l, x, mod, lay, 2, tm_mm, TN_OUT)

        h2, logits = norm_modulate(x, mod, norm2_g, l, lay, 3, 4, router_w)
        idx, gate = route_tokens(logits, router_b[l])
        row_tok, pos, plan = dispatch_plan(idx, TB_MOE)
        x_rows = jnp.take(h2, row_tok, axis=0)
        tb_sh = math.gcd(TB_MOE, t)
        one = jnp.full((1,), t // tb_sh, jnp.int32)
        shared_y = grouped_ffn(h2, (one, jnp.zeros((1,), jnp.int32), one), sh_w_gate[:, None], sh_w_up[:, None],
                               sh_w_down[:, None], l, tb_sh, tf_moe, tn_moe, bf16)
        y_rows = grouped_ffn(x_rows, plan, exp_w_gate, exp_w_up, exp_w_down, l, TB_MOE, tf_moe, tn_moe, bf16)
        moe_y = jnp.sum(jnp.take(y_rows, pos, axis=0).astype(f32) * gate[:, :, None], axis=1)
        x = gated_residual_sum(x, moe_y, shared_y, mod, l, lay, 5, TM_ROW)

    return (x[:lay.p].reshape(nb1, s1, d), x[lay.p:].reshape(nb2, s2, d))
```

```python
import functools
import math

import jax
import jax.numpy as jnp
import numpy as np
from jax import lax
from jax.experimental import pallas as pl
from jax.experimental.pallas import tpu as pltpu

DEPTH = 2
ATT_QK = 64
ATT_V = 2 * ATT_QK
ATT_HEADS = 12
ATT_W = ATT_HEADS * ATT_V
HY_W = 1024
HY_ORDER = 2
HY_GROUPS = 8
HY_EMB = 33
HY_TARGET = 1e-2
HY_FAST = 0.3
HY_SLOW = 1.5
ML_V = 256
ML_QK = ML_V // 2
ML_HEADS = 6
ML_W = ML_HEADS * ML_V
N_EXPERTS = 64
TOP_K = 8
N_GROUPS = 8
TOPK_GROUPS = 4
ROUTED_SCALE = 2.5
N_MOD = 6
EPS = 1e-6

LANES = 128
VMEM_LIMIT = 56 * 1024 * 1024

ML_CHUNK = 256
TB_MOE = 512
TF_MOE = 512
TN_MOE = 2048
TM_MM = 1024
TN_IN = 768
TN_OUT = 512
TM_ROW = 256
TQ_ATT = 512
TK_ATT = 512
TW_ATT = 2048
TM_DFT = 512
TN_DFT = 256

f32 = jnp.float32
bf16 = jnp.bfloat16
HI = lax.Precision.HIGHEST


def _cparams(*sem):
    return pltpu.CompilerParams(dimension_semantics=sem, vmem_limit_bytes=VMEM_LIMIT)


class _Layout:
    def __init__(self, nb1, s1, nb2, s2):
        self.nb1, self.s1, self.nb2, self.s2 = nb1, s1, nb2, s2
        self.p = nb1 * s1
        self.t = self.p + nb2 * s2
        self.nseq = nb1 + nb2

    def seq_of_row(self, r):
        return jnp.where(r < self.p, r // self.s1, self.nb1 + (r - self.p) // self.s2)

    def is_boundary(self, r):
        return jnp.where(r <= self.p, r % self.s1 == 0, (r - self.p) % self.s2 == 0)

    def groups(self):
        return ((0, self.s1, self.nb1), (self.p, self.s2, self.nb2))


def _ada_body(c_ref, w_ref, b_ref, o_ref):
    c = c_ref[...]
    s = c * jax.nn.sigmoid(c)
    o_ref[...] = jnp.dot(s, w_ref[...], preferred_element_type=f32, precision=HI) + b_ref[...]


def ada_modulation(c_pad, ada_w, ada_b, tn):
    r, d = c_pad.shape
    n = ada_w.shape[2]
    return pl.pallas_call(
        _ada_body,
        grid=(DEPTH, n // tn),
        in_specs=[pl.BlockSpec((r, d), lambda l, j: (0, 0)),
                  pl.BlockSpec((None, d, tn), lambda l, j: (l, 0, j)),
                  pl.BlockSpec((None, 1, tn), lambda l, j: (l, 0, j))],
        out_specs=pl.BlockSpec((None, r, tn), lambda l, j: (l, 0, j)),
        out_shape=jax.ShapeDtypeStruct((DEPTH, r, n), f32),
        compiler_params=_cparams("parallel", "parallel"),
        name="ada_modulation",
    )(c_pad, ada_w, ada_b.reshape(DEPTH, 1, n))


def _norm_mod_body(x_ref, mod_ref, g_ref, *rest, shift_idx, scale_idx, with_router):
    x = x_ref[...]
    ms = jnp.mean(x * x, axis=-1, keepdims=True)
    y = x * lax.rsqrt(ms + EPS) * g_ref[...]
    h = y * (1.0 + mod_ref[scale_idx:scale_idx + 1, :]) + mod_ref[shift_idx:shift_idx + 1, :]
    if with_router:
        rw_ref, h_ref, logit_ref = rest
        logit_ref[...] = jnp.dot(h, rw_ref[...], preferred_element_type=f32, precision=HI)
    else:
        (h_ref,) = rest
    h_ref[...] = h.astype(bf16)


def norm_modulate(x, mod, gain, l, lay, shift_idx, scale_idx, router_w=None):
    t, d = x.shape
    tm = TM_ROW
    seq = lambda i: lay.seq_of_row(i * tm)
    in_specs = [pl.BlockSpec((tm, d), lambda i: (i, 0)),
                pl.BlockSpec((None, None, N_MOD, d), lambda i: (l, seq(i), 0, 0)),
                pl.BlockSpec((None, 1, d), lambda i: (l, 0, 0))]
    args = [x, mod, gain.reshape(DEPTH, 1, d)]
    out_specs = [pl.BlockSpec((tm, d), lambda i: (i, 0))]
    out_shape = [jax.ShapeDtypeStruct((t, d), bf16)]
    if router_w is not None:
        e = router_w.shape[2]
        in_specs.append(pl.BlockSpec((None, d, e), lambda i: (l, 0, 0)))
        args.append(router_w)
        out_specs.append(pl.BlockSpec((tm, e), lambda i: (i, 0)))
        out_shape.append(jax.ShapeDtypeStruct((t, e), f32))
    return pl.pallas_call(
        functools.partial(_norm_mod_body, shift_idx=shift_idx, scale_idx=scale_idx,
                          with_router=router_w is not None),
        grid=(t // tm,), in_specs=in_specs, out_specs=out_specs, out_shape=out_shape,
        compiler_params=_cparams("parallel"),
        name="norm_modulate",
    )(*args)


def _mm_body(a_ref, b_ref, o_ref):
    o_ref[...] = jnp.dot(a_ref[...], b_ref[...], preferred_element_type=f32).astype(o_ref.dtype)


def matmul(a, b, l, n_out, tm, tn, out_dtype=f32):
    m, k = a.shape
    return pl.pallas_call(
        _mm_body,
        grid=(m // tm, n_out // tn),
        in_specs=[pl.BlockSpec((tm, k), lambda i, j: (i, 0)),
                  pl.BlockSpec((None, k, tn), lambda i, j: (l, 0, j))],
        out_specs=pl.BlockSpec((tm, tn), lambda i, j: (i, j)),
        out_shape=jax.ShapeDtypeStruct((m, n_out), out_dtype),
        compiler_params=_cparams("parallel", "parallel"),
        name="matmul",
    )(a, b)


def _mm_res_body(a_ref, b_ref, res_ref, mod_ref, o_ref, *, gate_idx):
    acc = jnp.dot(a_ref[...], b_ref[...], preferred_element_type=f32)
    o_ref[...] = res_ref[...] + mod_ref[gate_idx:gate_idx + 1, :] * acc


def matmul_gated_residual(a, b, l, res, mod, lay, gate_idx, tm, tn):
    m, k = a.shape
    n = res.shape[1]
    seq = lambda i: lay.seq_of_row(i * tm)
    return pl.pallas_call(
        functools.partial(_mm_res_body, gate_idx=gate_idx),
        grid=(m // tm, n // tn),
        in_specs=[pl.BlockSpec((tm, k), lambda i, j: (i, 0)),
                  pl.BlockSpec((None, k, tn), lambda i, j: (l, 0, j)),
                  pl.BlockSpec((tm, tn), lambda i, j: (i, j)),
                  pl.BlockSpec((None, None, N_MOD, tn), lambda i, j: (l, seq(i), 0, j))],
        out_specs=pl.BlockSpec((tm, tn), lambda i, j: (i, j)),
        out_shape=jax.ShapeDtypeStruct((m, n), f32),
        compiler_params=_cparams("parallel", "parallel"),
        name="matmul_gated_residual",
    )(a, b, res, mod)


POS_BLOCK = 1024
POS_LANE0 = ATT_QK
LOG2E = math.log2(math.e)


def _attn_pre_body(q_ref, k_ref, v_ref, g_ref, qn_ref, kn_ref, vb_ref, *, tm, lay):
    lane = lax.broadcasted_iota(jnp.int32, q_ref.shape, 1)
    lo = lane < ATT_QK

    def half_norm(x, g):
        sq = x * x
        s_lo = jnp.sum(jnp.where(lo, sq, 0.0), axis=-1, keepdims=True)
        s_hi = jnp.sum(jnp.where(lo, 0.0, sq), axis=-1, keepdims=True)
        inv = jnp.where(lo, lax.rsqrt(s_lo / ATT_QK + EPS), lax.rsqrt(s_hi / ATT_QK + EPS))
        return x * inv * g

    qn = half_norm(q_ref[...], g_ref[0:1, :]) * (ATT_QK ** -0.5 * LOG2E)
    kn = half_norm(k_ref[...], g_ref[1:2, :])
    row = pl.program_id(0) * tm + lax.broadcasted_iota(jnp.int32, q_ref.shape, 0)
    r = jnp.where(row < lay.p, row % lay.s1, (row - lay.p) % lay.s2) % POS_BLOCK
    hi = ((r // LANES) * LANES).astype(f32)
    lo_part = (r % LANES).astype(f32)
    pos = jnp.where((lane >= POS_LANE0) & (lane < POS_LANE0 + 3), hi,
                    jnp.where((lane >= POS_LANE0 + 3) & (lane < POS_LANE0 + 6), lo_part, 0.0))
    for mi in range(2):
        qm = qn if mi == 0 else pltpu.roll(qn, ATT_QK, axis=1)
        km = kn if mi == 0 else pltpu.roll(kn, ATT_QK, axis=1)
        qn_ref[mi] = jnp.where(lo, qm, 0.0).astype(bf16)
        kn_ref[mi] = jnp.where(lo, km, pos).astype(bf16)
    vb_ref[...] = v_ref[...].astype(bf16)


def attn_prepare(proj, qk_g2, lay, tm):
    t = proj.shape[0]
    hd = ATT_HEADS
    blk = lambda off: pl.BlockSpec((tm, ATT_V), lambda i, h: (i, off + h))
    qk_spec = pl.BlockSpec((None, 2, tm, ATT_V), lambda i, h: (h, 0, i, 0))
    qk_shp = jax.ShapeDtypeStruct((hd, 2, t, ATT_V), bf16)
    return pl.pallas_call(
        functools.partial(_attn_pre_body, tm=tm, lay=lay),
        grid=(t // tm, hd),
        in_specs=[blk(0), blk(hd), blk(2 * hd), pl.BlockSpec((2, ATT_V), lambda i, h: (0, 0))],
        out_specs=[qk_spec, qk_spec, pl.BlockSpec((None, tm, ATT_V), lambda i, h: (h, i, 0))],
        out_shape=[qk_shp, qk_shp, jax.ShapeDtypeStruct((hd, t, ATT_V), bf16)],
        compiler_params=_cparams("parallel", "parallel"),
        name="attn_prepare",
    )(proj, proj, proj, qk_g2)


def alibi_slope_lanes():
    slopes = 2.0 ** (-8.0 * np.arange(1, ATT_HEADS + 1) / ATT_HEADS) * LOG2E
    c = jnp.asarray(slopes, f32)
    c1 = c.astype(bf16)
    c2 = (c - c1.astype(f32)).astype(bf16)
    c3 = (c - c1.astype(f32) - c2.astype(f32)).astype(bf16)
    pieces = jnp.stack([c1, c2, c3, c1, c2, c3], axis=1)
    lanes = jnp.zeros((ATT_HEADS, 1, ATT_V), bf16).at[:, 0, POS_LANE0:POS_LANE0 + 6].set(pieces)
    return c, lanes


def _attn_body(scal_ref, q_ref, k_ref, v_ref, cl_ref, g_ref, o_ref, m_ref, l_ref, acc_ref, s_ref,
               *, tq, tk, tw, seq_len, out_scale):
    h, qi = pl.program_id(0), pl.program_id(1)
    c = scal_ref[h]
    qpos0 = (qi * tq) % seq_len
    ipos = (qpos0 + lax.broadcasted_iota(jnp.int32, (tq, 1), 0)).astype(f32)
    lane = lax.broadcasted_iota(jnp.int32, (tq, ATT_V), 1)
    pos_lane = (lane >= POS_LANE0) & (lane < POS_LANE0 + 6)
    nt = (((1,), (1,)), ((), ()))
    m_ref[...] = jnp.full(m_ref.shape, -jnp.inf, f32)
    l_ref[...] = jnp.zeros(l_ref.shape, f32)
    acc_ref[...] = jnp.zeros(acc_ref.shape, f32)
    cl = cl_ref[...]

    def wide_tile(wb, carry):
        w0 = pl.multiple_of(wb * tw, tw)
        for sub in range(tw // tk):
            j0 = w0 + sub * tk
            cols = slice(sub * tk, (sub + 1) * tk)
            is_left = j0 + tk <= qpos0
            is_right = j0 >= qpos0 + tq
            sgn = jnp.where(is_left, 1.0, jnp.where(is_right, -1.0, 0.0))
            off = (sgn * c) * (((j0 // POS_BLOCK) * POS_BLOCK).astype(f32) - ipos)
            for mi in range(2):
                qv = jnp.where(pos_lane, sgn.astype(cl.dtype) * cl, q_ref[mi])
                s_ref[mi, :, cols] = lax.dot_general(qv, k_ref[mi, pl.ds(j0, tk), :], nt,
                                                     preferred_element_type=f32) + off

            @pl.when(jnp.logical_not(is_left | is_right))
            def _():
                rc = lax.broadcasted_iota(jnp.int32, (tq, tk), 0) - lax.broadcasted_iota(jnp.int32, (tq, tk), 1)
                bias = c * jnp.abs(rc + (qpos0 - j0)).astype(f32)
                for mi in range(2):
                    s_ref[mi, :, cols] = s_ref[mi, :, cols] - bias

        vt = v_ref[pl.ds(w0, tw), :]
        for mi in range(2):
            s = s_ref[mi]
            m_prev = m_ref[mi]
            m_new = jnp.maximum(m_prev, jnp.max(s, axis=-1, keepdims=True))
            alpha = jnp.exp2(m_prev - m_new)
            p = jnp.exp2(s - m_new)
            l_ref[mi] = alpha * l_ref[mi] + jnp.sum(p, axis=-1, keepdims=True)
            acc_ref[mi] = alpha * acc_ref[mi] + jnp.dot(p.astype(bf16), vt, preferred_element_type=f32)
            m_ref[mi] = m_new
        return carry

    lax.fori_loop(0, seq_len // tw, wide_tile, 0)

    lam = scal_ref[ATT_HEADS]
    o = acc_ref[0] / l_ref[0] - lam * (acc_ref[1] / l_ref[1])
    ms = jnp.mean(o * o, axis=-1, keepdims=True)
    o_ref[...] = (o * lax.rsqrt(ms + EPS) * g_ref[...] * out_scale).astype(o_ref.dtype)


def attention_group(scal, c_lanes, qn, kn, vb, out_g, row_start, seq_len, nseq, tq, tk, tw, out_scale):
    tq, tk, tw = min(tq, seq_len), min(tk, seq_len), min(tw, seq_len)
    assert POS_BLOCK % tk == 0 and tw % tk == 0
    q0, s0 = row_start // tq, row_start // seq_len
    rows = nseq * seq_len
    return pl.pallas_call(
        functools.partial(_attn_body, tq=tq, tk=tk, tw=tw, seq_len=seq_len, out_scale=out_scale),
        grid=(ATT_HEADS, rows // tq),
        in_specs=[pl.BlockSpec(memory_space=pltpu.SMEM),
                  pl.BlockSpec((None, 2, tq, ATT_V), lambda h, qi: (h, 0, q0 + qi, 0)),
                  pl.BlockSpec((None, 2, seq_len, ATT_V), lambda h, qi: (h, 0, s0 + (qi * tq) // seq_len, 0)),
                  pl.BlockSpec((None, seq_len, ATT_V), lambda h, qi: (h, s0 + (qi * tq) // seq_len, 0)),
                  pl.BlockSpec((None, 1, ATT_V), lambda h, qi: (h, 0, 0)),
                  pl.BlockSpec((1, ATT_V), lambda h, qi: (0, 0))],
        out_specs=pl.BlockSpec((tq, ATT_V), lambda h, qi: (qi, h)),
        out_shape=jax.ShapeDtypeStruct((rows, ATT_W), bf16),
        scratch_shapes=[pltpu.VMEM((2, tq, 1), f32), pltpu.VMEM((2, tq, 1), f32),
                        pltpu.VMEM((2, tq, ATT_V), f32), pltpu.VMEM((2, tq, tw), f32)],
        compiler_params=_cparams("parallel", "parallel"),
        name="diff_attention",
    )(scal, qn, kn, vb, c_lanes, out_g)


def _mlstm_body(q_ref, k_ref, v_ref, gc_ref, gr_ref, h_ref, c_ref, n_ref, m_ref, *, lc, lay, backward):
    c = pl.program_id(0)
    nchunks = pl.num_programs(0)
    heads = ML_HEADS
    ceff = nchunks - 1 - c if backward else c
    r0 = ceff * lc
    reset = lay.is_boundary(r0 + lc) if backward else lay.is_boundary(r0)

    @pl.when(reset)
    def _():
        c_ref[...] = jnp.zeros(c_ref.shape, f32)
        n_ref[...] = jnp.zeros(n_ref.shape, f32)
        m_ref[...] = jnp.zeros(m_ref.shape, f32)

    row = lax.broadcasted_iota(jnp.int32, (lc, lc), 0)
    col = lax.broadcasted_iota(jnp.int32, (lc, lc), 1)
    mask = (col >= row) if backward else (col <= row)
    gcol = gc_ref[...]
    lane = lax.broadcasted_iota(jnp.int32, gcol.shape, 1)
    d = 1 if backward else 0

    def col_of(idx):
        return jnp.sum(jnp.where(lane == idx, gcol, 0.0), axis=-1, keepdims=True)

    for h in range(heads):
        ia = d * heads + h
        a_col, i_col = col_of(ia), col_of(2 * heads + ia)
        a_row = gr_ref[ia:ia + 1, :]
        i_row = gr_ref[2 * heads + ia:2 * heads + ia + 1, :]
        g11 = gr_ref[4 * heads + ia:4 * heads + ia + 1, 0:1]
        m11 = m_ref[h][:, 0:1]
        qf = q_ref[:, h * ML_QK:(h + 1) * ML_QK] * (ML_QK ** -0.5)
        kf = k_ref[:, h * ML_QK:(h + 1) * ML_QK]
        qb, kb = qf.astype(bf16), kf.astype(bf16)
        vb = v_ref[:, h * ML_V:(h + 1) * ML_V].astype(bf16)
        dmat = jnp.where(mask, a_col - a_row + i_row, -jnp.inf)
        m_inter = a_col + m11
        mj = jnp.maximum(jnp.max(dmat, axis=-1, keepdims=True), m_inter)
        p = jnp.exp(dmat - mj)
        s = lax.dot_general(qb, kb, (((1,), (1,)), ((), ())), preferred_element_type=f32)
        wts = p * s
        inter = jnp.exp(m_inter - mj)
        cmat = c_ref[h]
        num = (jnp.dot(wts.astype(bf16), vb, preferred_element_type=f32)
               + inter * jnp.dot(qb, cmat.astype(bf16), preferred_element_type=f32))
        nvec = n_ref[h]
        den = jnp.sum(wts, axis=-1, keepdims=True) + inter * jnp.sum(qf * nvec, axis=-1, keepdims=True)
        h_ref[:, h * ML_V:(h + 1) * ML_V] = num / jnp.maximum(jnp.abs(den), jnp.exp(-mj))
        ws = g11 - a_col + i_col
        m_new = jnp.maximum(g11 + m11, jnp.max(ws, axis=0, keepdims=True))
        dec = jnp.exp(g11 + m11 - m_new)
        ek = jnp.exp(ws - m_new) * kf
        c_ref[h] = dec * cmat + lax.dot_general(ek.astype(bf16), vb, (((0,), (0,)), ((), ())),
                                                preferred_element_type=f32)
        n_ref[h] = dec * nvec + jnp.sum(ek, axis=0, keepdims=True)
        m_ref[h] = jnp.broadcast_to(m_new, (1, LANES))


def mlstm_scan(proj, gcol, grow, lay, lc, backward, q_off, k_off, v_off):
    t = proj.shape[0]
    nchunks = t // lc
    hq, hv = ML_HEADS * ML_QK, ML_W
    cidx = (lambda c: nchunks - 1 - c) if backward else (lambda c: c)
    return pl.pallas_call(
        functools.partial(_mlstm_body, lc=lc, lay=lay, backward=backward),
        grid=(nchunks,),
        in_specs=[pl.BlockSpec((lc, hq), lambda c: (cidx(c), q_off // hq)),
                  pl.BlockSpec((lc, hq), lambda c: (cidx(c), k_off // hq)),
                  pl.BlockSpec((lc, hv), lambda c: (cidx(c), v_off // hv)),
                  pl.BlockSpec((lc, LANES), lambda c: (cidx(c), 0)),
                  pl.BlockSpec((grow.shape[0], lc), lambda c: (0, cidx(c)))],
        out_specs=pl.BlockSpec((lc, hv), lambda c: (cidx(c), 0)),
        out_shape=jax.ShapeDtypeStruct((t, hv), f32),
        scratch_shapes=[pltpu.VMEM((ML_HEADS, ML_QK, ML_V), f32), pltpu.VMEM((ML_HEADS, 1, ML_QK), f32),
                        pltpu.VMEM((ML_HEADS, 1, LANES), f32)],
        compiler_params=_cparams("arbitrary"),
        name="mlstm_bwd" if backward else "mlstm_fwd",
    )(proj, proj, proj, gcol, grow)


def _mlstm_post_body(hf_ref, hb_ref, o_ref, g_ref, y_ref):
    hsum = hf_ref[...] + hb_ref[...]
    ms = jnp.mean(hsum * hsum, axis=-1, keepdims=True)
    y_ref[...] = (jax.nn.sigmoid(o_ref[...]) * (hsum * lax.rsqrt(ms + EPS) * g_ref[...])).astype(y_ref.dtype)


def mlstm_post(hf, hb, proj, o_off, ml_g, l, tm):
    t = hf.shape[0]
    blk = lambda off: pl.BlockSpec((tm, ML_V), lambda i, h: (i, off + h))
    return pl.pallas_call(
        _mlstm_post_body,
        grid=(t // tm, ML_HEADS),
        in_specs=[blk(0), blk(0), blk(o_off // ML_V), pl.BlockSpec((None, 1, ML_V), lambda i, h: (l, 0, h))],
        out_specs=blk(0),
        out_shape=jax.ShapeDtypeStruct((t, ML_W), bf16),
        compiler_params=_cparams("parallel", "parallel"),
        name="mlstm_post",
    )(hf, hb, proj, ml_g.reshape(DEPTH, 1, ML_W))


def mlstm_gate_tables(gates, gate_b, lc):
    t = gates.shape[0]
    hd = ML_HEADS
    g = gates[:, :4 * hd].reshape(t, 4, hd) + gate_b.astype(f32)
    i_f, f_f, i_b, f_b = g[:, 0], g[:, 1], g[:, 2], g[:, 3]
    lf_f = jax.nn.log_sigmoid(f_f).reshape(t // lc, lc, hd)
    lf_b = jax.nn.log_sigmoid(f_b).reshape(t // lc, lc, hd)
    a_f = jnp.cumsum(lf_f, axis=1)
    a_b = jnp.flip(jnp.cumsum(jnp.flip(lf_b, axis=1), axis=1), axis=1)
    tot_f = jnp.broadcast_to(a_f[:, -1:, :], a_f.shape)
    tot_b = jnp.broadcast_to(a_b[:, :1, :], a_b.shape)
    flat = lambda a: a.reshape(t, hd)
    cols = jnp.concatenate([flat(a_f), flat(a_b), i_f, i_b, flat(tot_f), flat(tot_b)], axis=1)
    gcol = jnp.pad(cols, ((0, 0), (0, LANES - cols.shape[1])))
    grow = jnp.pad(cols.T, ((0, (-cols.shape[1]) % 8), (0, 0)))
    return gcol, grow


def _hyconv_body(c_ref, p_ref, n_ref, w_ref, b_ref, u_ref, ub_ref, *, tm, lay):
    r0 = pl.program_id(0) * tm
    x = c_ref[...]
    prev_row = jnp.where(lay.is_boundary(r0), 0.0, p_ref[7:8, :])
    next_row = jnp.where(lay.is_boundary(r0 + tm), 0.0, n_ref[0:1, :])
    rid = lax.broadcasted_iota(jnp.int32, x.shape, 0)
    xm = jnp.where(rid == 0, prev_row, pltpu.roll(x, 1, axis=0))
    xp = jnp.where(rid == tm - 1, next_row, pltpu.roll(x, tm - 1, axis=0))
    u = xm * w_ref[0:1, :] + x * w_ref[1:2, :] + xp * w_ref[2:3, :] + b_ref[...]
    u_ref[...] = u
    ub_ref[...] = u.astype(bf16)


def hyena_short_conv(proj, hy_off, conv_w, conv_b, l, lay, tm, tc):
    t = proj.shape[0]
    width = 3 * HY_W
    c0 = hy_off // tc
    nblk8 = t // 8
    return pl.pallas_call(
        functools.partial(_hyconv_body, tm=tm, lay=lay),
        grid=(t // tm, width // tc),
        in_specs=[pl.BlockSpec((tm, tc), lambda i, j: (i, c0 + j)),
                  pl.BlockSpec((8, tc), lambda i, j: (jnp.maximum(i * (tm // 8) - 1, 0), c0 + j)),
                  pl.BlockSpec((8, tc), lambda i, j: (jnp.minimum((i + 1) * (tm // 8), nblk8 - 1), c0 + j)),
                  pl.BlockSpec((None, 3, tc), lambda i, j: (l, 0, j)),
                  pl.BlockSpec((None, 1, tc), lambda i, j: (l, 0, j))],
        out_specs=[pl.BlockSpec((tm, tc), lambda i, j: (i, j))] * 2,
        out_shape=[jax.ShapeDtypeStruct((t, width), f32), jax.ShapeDtypeStruct((t, width), bf16)],
        compiler_params=_cparams("parallel", "parallel"),
        name="hyena_short_conv",
    )(proj, proj, proj, conv_w, conv_b.reshape(DEPTH, 1, width))


def hyena_time_filters(seq_len, w1, b1, freq, w2, b2, w3, b3):
    t = jnp.linspace(0.0, 1.0, seq_len, dtype=f32)[:, None]
    bands = (HY_EMB - 1) // 2
    fr = jnp.linspace(1e-4, bands - 1, bands, dtype=f32)
    ang = (2.0 * math.pi / seq_len) * jnp.arange(seq_len, dtype=f32)[:, None] * fr[None, :]
    z = jnp.concatenate([t, jnp.cos(ang), -jnp.sin(ang)], axis=-1)
    fq = freq.astype(f32)
    h = jnp.sin(fq[0] * (jnp.dot(z, w1, precision=HI) + b1))
    h = jnp.sin(fq[1] * (jnp.dot(h, w2, precision=HI) + b2))
    h = (jnp.dot(h, w3, precision=HI) + b3).reshape(seq_len, HY_ORDER, 2, HY_W)
    deltas = jnp.linspace(math.log(HY_TARGET) / HY_SLOW, math.log(HY_TARGET) / HY_FAST, HY_W, dtype=f32)
    h = h * jnp.exp(-t * jnp.abs(deltas))[:, None, None, :]
    hf, hb = h[:, :, 0], h[:, :, 1]
    first = (jnp.arange(seq_len) == 0)[:, None, None]
    hf = jnp.where(first, hf + hb, hf)
    hb = jnp.where(first, 0.0, hb)
    p = (hf + hb).reshape(seq_len, HY_ORDER * HY_W)
    q = (hf - hb).reshape(seq_len, HY_ORDER * HY_W)
    return p.astype(bf16), q.astype(bf16)


def dft_matrices(seq_len):
    k = jnp.arange(seq_len, dtype=jnp.int32)[:, None]

    def table(n):
        ang = ((k * n[None, :]) % (2 * seq_len)).astype(f32) * (math.pi / seq_len)
        return jnp.cos(ang), jnp.sin(ang)

    ca, sa = table(jnp.arange(seq_len // LANES, dtype=jnp.int32) * LANES)
    cb, sb = table(jnp.arange(LANES, dtype=jnp.int32))
    ca, sa, cb, sb = ca[:, :, None], sa[:, :, None], cb[:, None, :], sb[:, None, :]
    cos_m = (ca * cb - sa * sb).reshape(seq_len, seq_len).astype(bf16)
    nsin_m = (-(sa * cb + ca * sb)).reshape(seq_len, seq_len).astype(bf16)
    alt = jnp.where(jnp.arange(seq_len) % 2 == 0, 1.0, -1.0).astype(bf16)
    alt8 = jnp.zeros((8, seq_len), bf16).at[0].set(alt)
    return cos_m, nsin_m, alt8


def _hy_filter_body(fc_ref, fs_ref, alt_ref, p_ref, q_ref, hr_ref, hi_ref):
    p = p_ref[...]
    hr_ref[...] = jnp.dot(fc_ref[...], p, preferred_element_type=f32)
    hi = jnp.dot(fs_ref[...], q_ref[...], preferred_element_type=f32)
    nyq = jnp.dot(alt_ref[...], p, preferred_element_type=f32)[0:1, :]
    grow = pl.program_id(0) * hr_ref.shape[0] + lax.broadcasted_iota(jnp.int32, hi.shape, 0)
    hi_ref[...] = jnp.where(grow == 0, nyq, hi)


def hyena_filter_spectrum(cos_m, nsin_m, alt8, p, q, tm, tn):
    seq_len, n = p.shape
    a_spec = pl.BlockSpec((tm, seq_len), lambda i, j: (i, 0))
    b_spec = pl.BlockSpec((seq_len, tn), lambda i, j: (0, j))
    o_spec = pl.BlockSpec((tm, tn), lambda i, j: (i, j))
    shp = jax.ShapeDtypeStruct((seq_len, n), f32)
    return pl.pallas_call(
        _hy_filter_body,
        grid=(seq_len // tm, n // tn),
        in_specs=[a_spec, a_spec, pl.BlockSpec((8, seq_len), lambda i, j: (0, 0)), b_spec, b_spec],
        out_specs=[o_spec, o_spec], out_shape=[shp, shp],
        compiler_params=_cparams("parallel", "parallel"),
        name="hyena_filter_spectrum",
    )(cos_m, nsin_m, alt8, p, q)


def _hy_fwd_body(fc_ref, fs_ref, alt_ref, z_ref, hr_ref, hi_ref, yr_ref, yi_ref, *, n_fft):
    z = z_ref[...]
    vr = jnp.dot(fc_ref[...], z, preferred_element_type=f32)
    vi = jnp.dot(fs_ref[...], z, preferred_element_type=f32)
    nyq = jnp.dot(alt_ref[...], z, preferred_element_type=f32)[0:1, :]
    hr, hi = hr_ref[...], hi_ref[...]
    grow = pl.program_id(0) * vr.shape[0] + lax.broadcasted_iota(jnp.int32, vr.shape, 0)
    row0 = grow == 0
    vihi = jnp.where(row0, nyq, vi) * hi
    yr = vr * hr - jnp.where(row0, 0.0, vihi)
    yi = jnp.where(row0, vihi, vr * hi + vi * hr)
    scale = jnp.where(row0, 1.0 / n_fft, 2.0 / n_fft)
    yr_ref[...] = (yr * scale).astype(bf16)
    yi_ref[...] = (yi * scale).astype(bf16)


def hyena_fwd(cos_m, nsin_m, alt8, zb, z_col, hr, hi, order, row_start, seq_len, nseq, tm, tn):
    tm, tn = min(tm, seq_len), min(tn, HY_W)
    rows = nseq * seq_len
    s0, i0 = row_start // seq_len, seq_len // tm
    a_spec = pl.BlockSpec((tm, seq_len), lambda i, s, j: (i, 0))
    h_spec = pl.BlockSpec((tm, tn), lambda i, s, j: (i, order * (HY_W // tn) + j))
    o_spec = pl.BlockSpec((tm, tn), lambda i, s, j: (s * i0 + i, j))
    shp = jax.ShapeDtypeStruct((rows, HY_W), bf16)
    return pl.pallas_call(
        functools.partial(_hy_fwd_body, n_fft=2 * seq_len),
        grid=(seq_len // tm, nseq, HY_W // tn),
        in_specs=[a_spec, a_spec, pl.BlockSpec((8, seq_len), lambda i, s, j: (0, 0)),
                  pl.BlockSpec((seq_len, tn), lambda i, s, j: (s0 + s, z_col // tn + j)),
                  h_spec, h_spec],
        out_specs=[o_spec, o_spec], out_shape=[shp, shp],
        compiler_params=_cparams("parallel", "parallel", "parallel"),
        name="hyena_fwd_dft",
    )(cos_m, nsin_m, alt8, zb, hr, hi)


def _hy_inv_body(fc_ref, ft_ref, yr_ref, yi_ref, z_ref, x_ref, d_ref, *rest, last):
    y = (jnp.dot(fc_ref[...], yr_ref[...], preferred_element_type=f32)
         + jnp.dot(ft_ref[...], yi_ref[...], preferred_element_type=f32))
    n = pl.program_id(0) * y.shape[0] + lax.broadcasted_iota(jnp.int32, (y.shape[0], 1), 0)
    y = y + jnp.where(n % 2 == 0, 1.0, -1.0) * yi_ref[0:1, :].astype(f32)
    z = x_ref[...] * (y + d_ref[...] * z_ref[...])
    if last:
        g_ref, o_ref = rest
        gw = HY_W // HY_GROUPS
        for gi in range(z.shape[1] // gw):
            seg = z[:, gi * gw:(gi + 1) * gw]
            ms = jnp.mean(seg * seg, axis=-1, keepdims=True)
            o_ref[:, gi * gw:(gi + 1) * gw] = (seg * lax.rsqrt(ms + EPS)
                                               * g_ref[:, gi * gw:(gi + 1) * gw]).astype(o_ref.dtype)
    else:
        o_ref, ob_ref = rest
        o_ref[...] = z
        ob_ref[...] = z.astype(bf16)


def hyena_inv(cos_m, nsin_m, yr, yi, z_arr, z_row0, z_col, u, x_col, d_bias, gain, l, order,
              row_start, seq_len, nseq, tm, tn):
    tm, tn = min(tm, seq_len), min(tn, HY_W)
    last = order == HY_ORDER - 1
    rows = nseq * seq_len
    i0 = seq_len // tm
    a_spec = pl.BlockSpec((tm, seq_len), lambda i, s, j: (i, 0))
    b_spec = pl.BlockSpec((seq_len, tn), lambda i, s, j: (s, j))
    e_spec = lambda row0, col: pl.BlockSpec((tm, tn), lambda i, s, j: (row0 // tm + s * i0 + i, col // tn + j))
    o_spec = pl.BlockSpec((tm, tn), lambda i, s, j: (s * i0 + i, j))
    in_specs = [a_spec, a_spec, b_spec, b_spec, e_spec(z_row0, z_col), e_spec(row_start, x_col),
                pl.BlockSpec((None, 1, tn), lambda i, s, j: (l * HY_ORDER + order, 0, j))]
    args = [cos_m, nsin_m, yr, yi, z_arr, u, d_bias.reshape(DEPTH * HY_ORDER, 1, HY_W)]
    if last:
        in_specs.append(pl.BlockSpec((None, 1, tn), lambda i, s, j: (l, 0, j)))
        args.append(gain.reshape(DEPTH, 1, HY_W))
        out_specs = o_spec
        out_shape = jax.ShapeDtypeStruct((rows, HY_W), bf16)
    else:
        out_specs = [o_spec, o_spec]
        out_shape = [jax.ShapeDtypeStruct((rows, HY_W), f32), jax.ShapeDtypeStruct((rows, HY_W), bf16)]
    return pl.pallas_call(
        functools.partial(_hy_inv_body, last=last),
        grid=(seq_len // tm, nseq, HY_W // tn),
        in_specs=in_specs, out_specs=out_specs, out_shape=out_shape,
        compiler_params=_cparams("parallel", "parallel", "parallel"),
        name="hyena_inv_dft",
    )(*args)


def hyena_group(u, ub, spec, d_bias, gain, l, row_start, seq_len, nseq):
    cos_m, nsin_m, alt8, hr, hi = spec
    grp = (row_start, seq_len, nseq, TM_DFT, TN_DFT)
    yr, yi = hyena_fwd(cos_m, nsin_m, alt8, ub, 0, hr, hi, 0, *grp)
    z1, z1b = hyena_inv(cos_m, nsin_m, yr, yi, u, row_start, 0, u, HY_W, d_bias, gain, l, 0, *grp)
    yr, yi = hyena_fwd(cos_m, nsin_m, alt8, z1b, 0, hr, hi, 1, 0, seq_len, nseq, TM_DFT, TN_DFT)
    return hyena_inv(cos_m, nsin_m, yr, yi, z1, 0, 0, u, 2 * HY_W, d_bias, gain, l, 1, *grp)


def _item_changed(w, e_ref, c_ref):
    prev = jnp.maximum(w - 1, 0)
    return (w == 0) | (e_ref[w] != e_ref[prev]) | (c_ref[w] != c_ref[prev])


def _ffn_up_body(ib_ref, ic_ref, ie_ref, n_ref, x_ref, wg_ref, wu_ref, h_ref, wgb_ref, wub_ref):
    w = pl.program_id(0)

    @pl.when(w < n_ref[0])
    def _():
        @pl.when(_item_changed(w, ie_ref, ic_ref))
        def _():
            wgb_ref[...] = wg_ref[...].astype(bf16)
            wub_ref[...] = wu_ref[...].astype(bf16)

        x = x_ref[...]
        g = jnp.dot(x, wgb_ref[...], preferred_element_type=f32)
        u = jnp.dot(x, wub_ref[...], preferred_element_type=f32)
        h_ref[...] = (g * jax.nn.sigmoid(g) * u).astype(h_ref.dtype)


def _ffn_down_body(ib_ref, ic_ref, ie_ref, n_ref, h_ref, wd_ref, y_ref, wdb_ref):
    w = pl.program_id(0)

    @pl.when(w < n_ref[0])
    def _():
        @pl.when(_item_changed(w, ie_ref, ic_ref))
        def _():
            wdb_ref[...] = wd_ref[...].astype(bf16)

        y_ref[...] = jnp.dot(h_ref[...], wdb_ref[...], preferred_element_type=f32).astype(y_ref.dtype)


def ffn_items(plan, nb, ncol):
    blocks_per_e, first_block_e, n_used = plan
    n_items = n_used[0] * ncol
    w = jnp.minimum(jnp.arange(nb * ncol, dtype=jnp.int32), n_items - 1)
    istart = first_block_e * ncol
    e = jnp.sum((istart[None, :] <= w[:, None]).astype(jnp.int32), axis=1) - 1
    local = w - istart[e]
    cnt = jnp.maximum(blocks_per_e[e], 1)
    item_c = (local // cnt).astype(jnp.int32)
    item_b = (first_block_e[e] + local % cnt).astype(jnp.int32)
    return item_b, item_c, e, n_items[None].astype(jnp.int32)


def grouped_ffn(x_rows, plan, wg, wu, wd, l, tb, tf, tn, out_dtype):
    rows, d = x_rows.shape
    fdim = wg.shape[3]
    nb = rows // tb
    h = pl.pallas_call(
        _ffn_up_body,
        grid_spec=pltpu.PrefetchScalarGridSpec(
            num_scalar_prefetch=4, grid=(nb * (fdim // tf),),
            in_specs=[pl.BlockSpec((tb, d), lambda w, ib, ic, ie, n: (ib[w], 0)),
                      pl.BlockSpec((None, None, d, tf), lambda w, ib, ic, ie, n: (l, ie[w], 0, ic[w])),
                      pl.BlockSpec((None, None, d, tf), lambda w, ib, ic, ie, n: (l, ie[w], 0, ic[w]))],
            out_specs=pl.BlockSpec((tb, tf), lambda w, ib, ic, ie, n: (ib[w], ic[w])),
            scratch_shapes=[pltpu.VMEM((d, tf), bf16), pltpu.VMEM((d, tf), bf16)]),
        out_shape=jax.ShapeDtypeStruct((rows, fdim), bf16),
        compiler_params=_cparams("arbitrary"),
        name="ffn_up",
    )(*ffn_items(plan, nb, fdim // tf), x_rows, wg, wu)
    return pl.pallas_call(
        _ffn_down_body,
        grid_spec=pltpu.PrefetchScalarGridSpec(
            num_scalar_prefetch=4, grid=(nb * (d // tn),),
            in_specs=[pl.BlockSpec((tb, fdim), lambda w, ib, ic, ie, n: (ib[w], 0)),
                      pl.BlockSpec((None, None, fdim, tn), lambda w, ib, ic, ie, n: (l, ie[w], 0, ic[w]))],
            out_specs=pl.BlockSpec((tb, tn), lambda w, ib, ic, ie, n: (ib[w], ic[w])),
            scratch_shapes=[pltpu.VMEM((fdim, tn), bf16)]),
        out_shape=jax.ShapeDtypeStruct((rows, d), out_dtype),
        compiler_params=_cparams("arbitrary"),
        name="ffn_down",
    )(*ffn_items(plan, nb, d // tn), h, wd)


def route_tokens(logits, router_b):
    t = logits.shape[0]
    per = N_EXPERTS // N_GROUPS
    s = jax.nn.sigmoid(logits)
    sb = s + router_b.astype(f32)
    gscore = jnp.sum(lax.top_k(sb.reshape(t, N_GROUPS, per), 2)[0], axis=-1)
    _, gidx = lax.top_k(gscore, TOPK_GROUPS)
    gmask = jnp.sum(jax.nn.one_hot(gidx, N_GROUPS, dtype=f32), axis=1) > 0
    emask = jnp.repeat(gmask, per, axis=1)
    _, idx = lax.top_k(jnp.where(emask, sb, -jnp.inf), TOP_K)
    gate = jnp.take_along_axis(s, idx, axis=1)
    gate = gate / jnp.sum(gate, axis=-1, keepdims=True) * ROUTED_SCALE
    return idx, gate


def dispatch_plan(idx, tb):
    t = idx.shape[0]
    n = t * TOP_K
    e_flat = idx.reshape(n).astype(jnp.int32)
    onehot = (e_flat[:, None] == jnp.arange(N_EXPERTS, dtype=jnp.int32)[None, :]).astype(jnp.int32)
    csum = jnp.cumsum(onehot, axis=0)
    counts = csum[-1]
    rank = jnp.take_along_axis(csum, e_flat[:, None], axis=1)[:, 0] - 1
    padded = (counts + tb - 1) // tb * tb
    pends = jnp.cumsum(padded)
    pstarts = pends - padded
    pos = pstarts[e_flat] + rank
    nb = (n + N_EXPERTS * (tb - 1) + tb - 1) // tb
    tok = jnp.arange(n, dtype=jnp.int32) // TOP_K
    row_tok = jnp.zeros((nb * tb,), jnp.int32).at[pos].set(tok, unique_indices=True)
    plan = ((padded // tb).astype(jnp.int32), (pstarts // tb).astype(jnp.int32),
            (pends[-1:] // tb).astype(jnp.int32))
    return row_tok, pos.reshape(t, TOP_K), plan


def _final_body(x_ref, moe_ref, sh_ref, mod_ref, o_ref, *, gate_idx):
    o_ref[...] = x_ref[...] + mod_ref[gate_idx:gate_idx + 1, :] * (moe_ref[...] + sh_ref[...])


def gated_residual_sum(x, moe_y, shared_y, mod, l, lay, gate_idx, tm):
    t, d = x.shape
    seq = lambda i: lay.seq_of_row(i * tm)
    blk = pl.BlockSpec((tm, d), lambda i: (i, 0))
    return pl.pallas_call(
        functools.partial(_final_body, gate_idx=gate_idx),
        grid=(t // tm,),
        in_specs=[blk, blk, blk, pl.BlockSpec((None, None, N_MOD, d), lambda i: (l, seq(i), 0, 0))],
        out_specs=blk, out_shape=jax.ShapeDtypeStruct((t, d), f32),
        compiler_params=_cparams("parallel"),
        name="gated_residual_sum",
    )(x, moe_y, shared_y, mod)


def _col_offsets():
    sizes = [ATT_W, ATT_W, ATT_W, 3 * HY_W, ML_HEADS * ML_QK, ML_HEADS * ML_QK, ML_W, ML_W]
    return np.concatenate([[0], np.cumsum(sizes)]).tolist()


def kernel(x_prompt, x_sample, c_prompt, c_sample, ada_w, ada_b, norm1_g, norm2_g, w_in, w_out, qk_norm_g,
           diff_lambda, attn_out_g, hy_conv_w, hy_conv_b, hy_f_w1, hy_f_b1, hy_f_freq, hy_f_w2, hy_f_b2,
           hy_f_w3, hy_f_b3, hy_bias, hy_out_g, ml_gate_b, ml_out_g, router_w, router_b, exp_w_gate,
           exp_w_up, exp_w_down, sh_w_gate, sh_w_up, sh_w_down):
    nb1, s1, d = x_prompt.shape
    nb2, s2, _ = x_sample.shape
    lay = _Layout(nb1, s1, nb2, s2)
    t = lay.t
    offs = _col_offsets()
    n_main = offs[-1]
    q_off, k_off, v_off, hy_off, mq_off, mk_off, mv_off, mo_off = offs[:8]
    assert HY_W // HY_GROUPS == LANES and ATT_V == LANES

    x = jnp.concatenate([x_prompt.reshape(nb1 * s1, d), x_sample.reshape(nb2 * s2, d)], axis=0)
    c_all = jnp.concatenate([c_prompt, c_sample], axis=0).astype(f32)
    c_pad = jnp.pad(c_all, ((0, (-lay.nseq) % 8), (0, 0)))
    mod = ada_modulation(c_pad, ada_w, ada_b, tn=min(1024, d))
    mod = mod.reshape(DEPTH, c_pad.shape[0], N_MOD, d)

    w_in_b = w_in.astype(bf16)
    w_gate_b = jnp.pad(w_in[:, :, n_main:], ((0, 0), (0, 0), (0, LANES - (w_in.shape[2] - n_main)))).astype(bf16)
    w_out_b = w_out.astype(bf16)
    tf_moe, tn_moe = min(TF_MOE, exp_w_gate.shape[3]), min(TN_MOE, d)
    dft = {sl: dft_matrices(sl) for sl in sorted({s1, s2})}
    slopes, slope_lanes = alibi_slope_lanes()

    tm_mm = math.gcd(TM_MM, s2)
    for l in range(DEPTH):
        lam_init = 0.8 - 0.6 * math.exp(-0.3 * l)
        h1 = norm_modulate(x, mod, norm1_g, l, lay, 0, 1)[0]
        proj = matmul(h1, w_in_b, l, n_main, tm_mm, TN_IN)
        gates = matmul(h1, w_gate_b, l, LANES, tm_mm, LANES)

        lp = diff_lambda[l].astype(f32)
        lam = jnp.exp(jnp.sum(lp[0] * lp[1])) - jnp.exp(jnp.sum(lp[2] * lp[3])) + lam_init
        scal = jnp.concatenate([slopes, lam[None], jnp.zeros((3,), f32)])
        qn, kn, vb = attn_prepare(proj, qk_norm_g[l].reshape(2, ATT_V).astype(f32), lay, tm_mm)
        out_g = attn_out_g[l].reshape(1, ATT_V).astype(f32)
        ya = jnp.concatenate([
            attention_group(scal, slope_lanes, qn, kn, vb, out_g, rs, sl, ns, TQ_ATT, TK_ATT, TW_ATT, 1.0 - lam_init)
            for rs, sl, ns in lay.groups()], axis=0)

        u, ub = hyena_short_conv(proj, hy_off, hy_conv_w, hy_conv_b, l, lay, TM_ROW, math.gcd(hy_off, HY_W, 512))
        yb_parts = []
        for rs, sl, ns in lay.groups():
            cos_m, nsin_m, alt8 = dft[sl]
            p, q = hyena_time_filters(sl, hy_f_w1[l], hy_f_b1[l], hy_f_freq[l], hy_f_w2[l], hy_f_b2[l],
                                      hy_f_w3[l], hy_f_b3[l])
            hr, hi = hyena_filter_spectrum(cos_m, nsin_m, alt8, p, q, min(TM_DFT, sl), TN_DFT)
            yb_parts.append(hyena_group(u, ub, (cos_m, nsin_m, alt8, hr, hi), hy_bias, hy_out_g, l, rs, sl, ns))
        yb = jnp.concatenate(yb_parts, axis=0)

        gcol, grow = mlstm_gate_tables(gates, ml_gate_b[l], ML_CHUNK)
        hf = mlstm_scan(proj, gcol, grow, lay, ML_CHUNK, False, mq_off, mk_off, mv_off)
        hb = mlstm_scan(proj, gcol, grow, lay, ML_CHUNK, True, mq_off, mk_off, mv_off)
        yc = mlstm_post(hf, hb, proj, mo_off, ml_out_g, l, tm_mm)

        ycat = jnp.concatenate([ya, yb, yc], axis=1)
        x = matmul_gated_residual(ycat, w_out_b, l, x, mod, lay, 2, tm_mm, TN_OUT)

        h2, logits = norm_modulate(x, mod, norm2_g, l, lay, 3, 4, router_w)
        routed = [route_tokens(logits[rs:rs + sl * ns], router_b[l]) for rs, sl, ns in lay.groups()]
        idx = jnp.concatenate([r[0] for r in routed], axis=0)
        gate = jnp.concatenate([r[1] for r in routed], axis=0)
        row_tok, pos, plan = dispatch_plan(idx, TB_MOE)
        x_rows = jnp.take(h2, row_tok, axis=0)
        tb_sh = math.gcd(TB_MOE, t)
        one = jnp.full((1,), t // tb_sh, jnp.int32)
        shared_y = grouped_ffn(h2, (one, jnp.zeros((1,), jnp.int32), one), sh_w_gate[:, None], sh_w_up[:, None],
                               sh_w_down[:, None], l, tb_sh, tf_moe, tn_moe, bf16)
        y_rows = grouped_ffn(x_rows, plan, exp_w_gate, exp_w_up, exp_w_down, l, TB_MOE, tf_moe, tn_moe, bf16)
        moe_y = jnp.sum(jnp.take(y_rows, pos, axis=0).astype(f32) * gate[:, :, None], axis=1)
        x = gated_residual_sum(x, moe_y, shared_y, mod, l, lay, 5, TM_ROW)

    return (x[:lay.p].reshape(nb1, s1, d), x[lay.p:].reshape(nb2, s2, d))
```

```python
import functools
import math

import jax
import jax.numpy as jnp
import numpy as np
from jax import lax
from jax.experimental import pallas as pl
from jax.experimental.pallas import tpu as pltpu

DEPTH = 2
ATT_QK = 64
ATT_V = 2 * ATT_QK
ATT_HEADS = 12
ATT_W = ATT_HEADS * ATT_V
HY_W = 1024
HY_ORDER = 2
HY_GROUPS = 8
HY_EMB = 33
HY_TARGET = 1e-2
HY_FAST = 0.3
HY_SLOW = 1.5
ML_V = 256
ML_QK = ML_V // 2
ML_HEADS = 6
ML_W = ML_HEADS * ML_V
N_EXPERTS = 64
TOP_K = 8
N_GROUPS = 8
TOPK_GROUPS = 4
ROUTED_SCALE = 2.5
N_MOD = 6
EPS = 1e-6

LANES = 128
VMEM_LIMIT = 56 * 1024 * 1024

ML_CHUNK = 256
TB_MOE = 512
TF_MOE = 512
TN_MOE = 2048
TM_MM = 1024
TN_IN = 768
TN_OUT = 512
TM_ROW = 256
TQ_ATT = 512
TK_ATT = 512
TW_ATT = 2048
TM_DFT = 512
TN_DFT = 256

f32 = jnp.float32
bf16 = jnp.bfloat16
HI = lax.Precision.HIGHEST


def _cparams(*sem):
    return pltpu.CompilerParams(dimension_semantics=sem, vmem_limit_bytes=VMEM_LIMIT)


class _Layout:
    def __init__(self, nb1, s1, nb2, s2):
        self.nb1, self.s1, self.nb2, self.s2 = nb1, s1, nb2, s2
        self.p = nb1 * s1
        self.t = self.p + nb2 * s2
        self.nseq = nb1 + nb2

    def seq_of_row(self, r):
        return jnp.where(r < self.p, r // self.s1, self.nb1 + (r - self.p) // self.s2)

    def is_boundary(self, r):
        return jnp.where(r <= self.p, r % self.s1 == 0, (r - self.p) % self.s2 == 0)

    def groups(self):
        return ((0, self.s1, self.nb1), (self.p, self.s2, self.nb2))


def _ada_body(c_ref, w_ref, b_ref, o_ref):
    c = c_ref[...]
    s = c * jax.nn.sigmoid(c)
    o_ref[...] = jnp.dot(s, w_ref[...], preferred_element_type=f32, precision=HI) + b_ref[...]


def ada_modulation(c_pad, ada_w, ada_b, tn):
    r, d = c_pad.shape
    n = ada_w.shape[2]
    return pl.pallas_call(
        _ada_body,
        grid=(DEPTH, n // tn),
        in_specs=[pl.BlockSpec((r, d), lambda l, j: (0, 0)),
                  pl.BlockSpec((None, d, tn), lambda l, j: (l, 0, j)),
                  pl.BlockSpec((None, 1, tn), lambda l, j: (l, 0, j))],
        out_specs=pl.BlockSpec((None, r, tn), lambda l, j: (l, 0, j)),
        out_shape=jax.ShapeDtypeStruct((DEPTH, r, n), f32),
        compiler_params=_cparams("parallel", "parallel"),
        name="ada_modulation",
    )(c_pad, ada_w, ada_b.reshape(DEPTH, 1, n))


def _norm_mod_body(x_ref, mod_ref, g_ref, *rest, shift_idx, scale_idx, with_router):
    x = x_ref[...]
    ms = jnp.mean(x * x, axis=-1, keepdims=True)
    y = x * lax.rsqrt(ms + EPS) * g_ref[...]
    h = y * (1.0 + mod_ref[scale_idx:scale_idx + 1, :]) + mod_ref[shift_idx:shift_idx + 1, :]
    if with_router:
        rw_ref, h_ref, hf_ref, logit_ref = rest
        logit_ref[...] = jnp.dot(h, rw_ref[...], preferred_element_type=f32, precision=HI)
        hf_ref[...] = h
    else:
        (h_ref,) = rest
    h_ref[...] = h.astype(bf16)


def norm_modulate(x, mod, gain, l, lay, shift_idx, scale_idx, router_w=None):
    t, d = x.shape
    tm = TM_ROW
    seq = lambda i: lay.seq_of_row(i * tm)
    in_specs = [pl.BlockSpec((tm, d), lambda i: (i, 0)),
                pl.BlockSpec((None, None, N_MOD, d), lambda i: (l, seq(i), 0, 0)),
                pl.BlockSpec((None, 1, d), lambda i: (l, 0, 0))]
    args = [x, mod, gain.reshape(DEPTH, 1, d)]
    out_specs = [pl.BlockSpec((tm, d), lambda i: (i, 0))]
    out_shape = [jax.ShapeDtypeStruct((t, d), bf16)]
    if router_w is not None:
        e = router_w.shape[2]
        in_specs.append(pl.BlockSpec((None, d, e), lambda i: (l, 0, 0)))
        args.append(router_w)
        out_specs += [pl.BlockSpec((tm, d), lambda i: (i, 0)), pl.BlockSpec((tm, e), lambda i: (i, 0))]
        out_shape += [jax.ShapeDtypeStruct((t, d), f32), jax.ShapeDtypeStruct((t, e), f32)]
    return pl.pallas_call(
        functools.partial(_norm_mod_body, shift_idx=shift_idx, scale_idx=scale_idx,
                          with_router=router_w is not None),
        grid=(t // tm,), in_specs=in_specs, out_specs=out_specs, out_shape=out_shape,
        compiler_params=_cparams("parallel"),
        name="norm_modulate",
    )(*args)


def _mm_body(a_ref, b_ref, o_ref):
    o_ref[...] = jnp.dot(a_ref[...], b_ref[...], preferred_element_type=f32).astype(o_ref.dtype)


def matmul(a, b, l, n_out, tm, tn, out_dtype=f32):
    m, k = a.shape
    return pl.pallas_call(
        _mm_body,
        grid=(m // tm, n_out // tn),
        in_specs=[pl.BlockSpec((tm, k), lambda i, j: (i, 0)),
                  pl.BlockSpec((None, k, tn), lambda i, j: (l, 0, j))],
        out_specs=pl.BlockSpec((tm, tn), lambda i, j: (i, j)),
        out_shape=jax.ShapeDtypeStruct((m, n_out), out_dtype),
        compiler_params=_cparams("parallel", "parallel"),
        name="matmul",
    )(a, b)


def _mm_res_body(a_ref, b_ref, res_ref, mod_ref, o_ref, *, gate_idx):
    acc = jnp.dot(a_ref[...], b_ref[...], preferred_element_type=f32)
    o_ref[...] = res_ref[...] + mod_ref[gate_idx:gate_idx + 1, :] * acc


def matmul_gated_residual(a, b, l, res, mod, lay, gate_idx, tm, tn):
    m, k = a.shape
    n = res.shape[1]
    seq = lambda i: lay.seq_of_row(i * tm)
    return pl.pallas_call(
        functools.partial(_mm_res_body, gate_idx=gate_idx),
        grid=(m // tm, n // tn),
        in_specs=[pl.BlockSpec((tm, k), lambda i, j: (i, 0)),
                  pl.BlockSpec((None, k, tn), lambda i, j: (l, 0, j)),
                  pl.BlockSpec((tm, tn), lambda i, j: (i, j)),
                  pl.BlockSpec((None, None, N_MOD, tn), lambda i, j: (l, seq(i), 0, j))],
        out_specs=pl.BlockSpec((tm, tn), lambda i, j: (i, j)),
        out_shape=jax.ShapeDtypeStruct((m, n), f32),
        compiler_params=_cparams("parallel", "parallel"),
        name="matmul_gated_residual",
    )(a, b, res, mod)


POS_BLOCK = 1024
POS_LANE0 = ATT_QK
LOG2E = math.log2(math.e)


def _attn_pre_body(q_ref, k_ref, v_ref, g_ref, qn_ref, kn_ref, vb_ref, *, tm, lay):
    lane = lax.broadcasted_iota(jnp.int32, q_ref.shape, 1)
    lo = lane < ATT_QK

    def half_norm(x, g):
        sq = x * x
        s_lo = jnp.sum(jnp.where(lo, sq, 0.0), axis=-1, keepdims=True)
        s_hi = jnp.sum(jnp.where(lo, 0.0, sq), axis=-1, keepdims=True)
        inv = jnp.where(lo, lax.rsqrt(s_lo / ATT_QK + EPS), lax.rsqrt(s_hi / ATT_QK + EPS))
        return x * inv * g

    qn = half_norm(q_ref[...], g_ref[0:1, :]) * (ATT_QK ** -0.5 * LOG2E)
    kn = half_norm(k_ref[...], g_ref[1:2, :])
    row = pl.program_id(0) * tm + lax.broadcasted_iota(jnp.int32, q_ref.shape, 0)
    r = jnp.where(row < lay.p, row % lay.s1, (row - lay.p) % lay.s2) % POS_BLOCK
    hi = ((r // LANES) * LANES).astype(f32)
    lo_part = (r % LANES).astype(f32)
    pos = jnp.where((lane >= POS_LANE0) & (lane < POS_LANE0 + 3), hi,
                    jnp.where((lane >= POS_LANE0 + 3) & (lane < POS_LANE0 + 6), lo_part, 0.0))
    for mi in range(2):
        qm = qn if mi == 0 else pltpu.roll(qn, ATT_QK, axis=1)
        km = kn if mi == 0 else pltpu.roll(kn, ATT_QK, axis=1)
        qn_ref[mi] = jnp.where(lo, qm, 0.0).astype(bf16)
        kn_ref[mi] = jnp.where(lo, km, pos).astype(bf16)
    vb_ref[...] = v_ref[...].astype(bf16)


def attn_prepare(proj, qk_g2, lay, tm):
    t = proj.shape[0]
    hd = ATT_HEADS
    blk = lambda off: pl.BlockSpec((tm, ATT_V), lambda i, h: (i, off + h))
    qk_spec = pl.BlockSpec((None, 2, tm, ATT_V), lambda i, h: (h, 0, i, 0))
    qk_shp = jax.ShapeDtypeStruct((hd, 2, t, ATT_V), bf16)
    return pl.pallas_call(
        functools.partial(_attn_pre_body, tm=tm, lay=lay),
        grid=(t // tm, hd),
        in_specs=[blk(0), blk(hd), blk(2 * hd), pl.BlockSpec((2, ATT_V), lambda i, h: (0, 0))],
        out_specs=[qk_spec, qk_spec, pl.BlockSpec((None, tm, ATT_V), lambda i, h: (h, i, 0))],
        out_shape=[qk_shp, qk_shp, jax.ShapeDtypeStruct((hd, t, ATT_V), bf16)],
        compiler_params=_cparams("parallel", "parallel"),
        name="attn_prepare",
    )(proj, proj, proj, qk_g2)


def alibi_slope_lanes():
    slopes = 2.0 ** (-8.0 * np.arange(1, ATT_HEADS + 1) / ATT_HEADS) * LOG2E
    c = jnp.asarray(slopes, f32)
    c1 = c.astype(bf16)
    c2 = (c - c1.astype(f32)).astype(bf16)
    c3 = (c - c1.astype(f32) - c2.astype(f32)).astype(bf16)
    pieces = jnp.stack([c1, c2, c3, c1, c2, c3], axis=1)
    lanes = jnp.zeros((ATT_HEADS, 1, ATT_V), bf16).at[:, 0, POS_LANE0:POS_LANE0 + 6].set(pieces)
    return c, lanes


def _attn_body(scal_ref, q_ref, k_ref, v_ref, cl_ref, g_ref, o_ref, m_ref, l_ref, acc_ref, s_ref,
               *, tq, tk, tw, seq_len, out_scale):
    h, qi = pl.program_id(0), pl.program_id(1)
    c = scal_ref[h]
    qpos0 = (qi * tq) % seq_len
    ipos = (qpos0 + lax.broadcasted_iota(jnp.int32, (tq, 1), 0)).astype(f32)
    lane = lax.broadcasted_iota(jnp.int32, (tq, ATT_V), 1)
    pos_lane = (lane >= POS_LANE0) & (lane < POS_LANE0 + 6)
    nt = (((1,), (1,)), ((), ()))
    m_ref[...] = jnp.full(m_ref.shape, -jnp.inf, f32)
    l_ref[...] = jnp.zeros(l_ref.shape, f32)
    acc_ref[...] = jnp.zeros(acc_ref.shape, f32)
    cl = cl_ref[...]

    def wide_tile(wb, carry):
        w0 = pl.multiple_of(wb * tw, tw)
        for sub in range(tw // tk):
            j0 = w0 + sub * tk
            cols = slice(sub * tk, (sub + 1) * tk)
            is_left = j0 + tk <= qpos0
            is_right = j0 >= qpos0 + tq
            sgn = jnp.where(is_left, 1.0, jnp.where(is_right, -1.0, 0.0))
            off = (sgn * c) * (((j0 // POS_BLOCK) * POS_BLOCK).astype(f32) - ipos)
            for mi in range(2):
                qv = jnp.where(pos_lane, sgn.astype(cl.dtype) * cl, q_ref[mi])
                s_ref[mi, :, cols] = lax.dot_general(qv, k_ref[mi, pl.ds(j0, tk), :], nt,
                                                     preferred_element_type=f32) + off

            @pl.when(jnp.logical_not(is_left | is_right))
            def _():
                rc = lax.broadcasted_iota(jnp.int32, (tq, tk), 0) - lax.broadcasted_iota(jnp.int32, (tq, tk), 1)
                bias = c * jnp.abs(rc + (qpos0 - j0)).astype(f32)
                for mi in range(2):
                    s_ref[mi, :, cols] = s_ref[mi, :, cols] - bias

        vt = v_ref[pl.ds(w0, tw), :]
        for mi in range(2):
            s = s_ref[mi]
            m_prev = m_ref[mi]
            m_new = jnp.maximum(m_prev, jnp.max(s, axis=-1, keepdims=True))
            alpha = jnp.exp2(m_prev - m_new)
            p = jnp.exp2(s - m_new)
            l_ref[mi] = alpha * l_ref[mi] + jnp.sum(p, axis=-1, keepdims=True)
            acc_ref[mi] = alpha * acc_ref[mi] + jnp.dot(p.astype(bf16), vt, preferred_element_type=f32)
            m_ref[mi] = m_new
        return carry

    lax.fori_loop(0, seq_len // tw, wide_tile, 0)

    lam = scal_ref[ATT_HEADS]
    o = acc_ref[0] / l_ref[0] - lam * (acc_ref[1] / l_ref[1])
    ms = jnp.mean(o * o, axis=-1, keepdims=True)
    o_ref[...] = (o * lax.rsqrt(ms + EPS) * g_ref[...] * out_scale).astype(o_ref.dtype)


def attention_group(scal, c_lanes, qn, kn, vb, out_g, row_start, seq_len, nseq, tq, tk, tw, out_scale):
    tq, tk, tw = min(tq, seq_len), min(tk, seq_len), min(tw, seq_len)
    assert POS_BLOCK % tk == 0 and tw % tk == 0
    q0, s0 = row_start // tq, row_start // seq_len
    rows = nseq * seq_len
    return pl.pallas_call(
        functools.partial(_attn_body, tq=tq, tk=tk, tw=tw, seq_len=seq_len, out_scale=out_scale),
        grid=(ATT_HEADS, rows // tq),
        in_specs=[pl.BlockSpec(memory_space=pltpu.SMEM),
                  pl.BlockSpec((None, 2, tq, ATT_V), lambda h, qi: (h, 0, q0 + qi, 0)),
                  pl.BlockSpec((None, 2, seq_len, ATT_V), lambda h, qi: (h, 0, s0 + (qi * tq) // seq_len, 0)),
                  pl.BlockSpec((None, seq_len, ATT_V), lambda h, qi: (h, s0 + (qi * tq) // seq_len, 0)),
                  pl.BlockSpec((None, 1, ATT_V), lambda h, qi: (h, 0, 0)),
                  pl.BlockSpec((1, ATT_V), lambda h, qi: (0, 0))],
        out_specs=pl.BlockSpec((tq, ATT_V), lambda h, qi: (qi, h)),
        out_shape=jax.ShapeDtypeStruct((rows, ATT_W), bf16),
        scratch_shapes=[pltpu.VMEM((2, tq, 1), f32), pltpu.VMEM((2, tq, 1), f32),
                        pltpu.VMEM((2, tq, ATT_V), f32), pltpu.VMEM((2, tq, tw), f32)],
        compiler_params=_cparams("parallel", "parallel"),
        name="diff_attention",
    )(scal, qn, kn, vb, c_lanes, out_g)


def _mlstm_body(q_ref, k_ref, v_ref, gc_ref, gr_ref, h_ref, c_ref, n_ref, m_ref, *, lc, lay, backward):
    c = pl.program_id(0)
    nchunks = pl.num_programs(0)
    heads = ML_HEADS
    ceff = nchunks - 1 - c if backward else c
    r0 = ceff * lc
    reset = lay.is_boundary(r0 + lc) if backward else lay.is_boundary(r0)

    @pl.when(reset)
    def _():
        c_ref[...] = jnp.zeros(c_ref.shape, f32)
        n_ref[...] = jnp.zeros(n_ref.shape, f32)
        m_ref[...] = jnp.zeros(m_ref.shape, f32)

    row = lax.broadcasted_iota(jnp.int32, (lc, lc), 0)
    col = lax.broadcasted_iota(jnp.int32, (lc, lc), 1)
    mask = (col >= row) if backward else (col <= row)
    gcol = gc_ref[...]
    lane = lax.broadcasted_iota(jnp.int32, gcol.shape, 1)
    d = 1 if backward else 0

    def col_of(idx):
        return jnp.sum(jnp.where(lane == idx, gcol, 0.0), axis=-1, keepdims=True)

    for h in range(heads):
        ia = d * heads + h
        a_col, i_col = col_of(ia), col_of(2 * heads + ia)
        a_row = gr_ref[ia:ia + 1, :]
        i_row = gr_ref[2 * heads + ia:2 * heads + ia + 1, :]
        g11 = gr_ref[4 * heads + ia:4 * heads + ia + 1, 0:1]
        m11 = m_ref[h][:, 0:1]
        qf = q_ref[:, h * ML_QK:(h + 1) * ML_QK] * (ML_QK ** -0.5)
        kf = k_ref[:, h * ML_QK:(h + 1) * ML_QK]
        qb, kb = qf.astype(bf16), kf.astype(bf16)
        vb = v_ref[:, h * ML_V:(h + 1) * ML_V].astype(bf16)
        dmat = jnp.where(mask, a_col - a_row + i_row, -jnp.inf)
        m_inter = a_col + m11
        mj = jnp.maximum(jnp.max(dmat, axis=-1, keepdims=True), m_inter)
        p = jnp.exp(dmat - mj)
        s = lax.dot_general(qb, kb, (((1,), (1,)), ((), ())), preferred_element_type=f32)
        wts = p * s
        inter = jnp.exp(m_inter - mj)
        cmat = c_ref[h]
        num = (jnp.dot(wts.astype(bf16), vb, preferred_element_type=f32)
               + inter * jnp.dot(qb, cmat.astype(bf16), preferred_element_type=f32))
        nvec = n_ref[h]
        den = jnp.sum(wts, axis=-1, keepdims=True) + inter * jnp.sum(qf * nvec, axis=-1, keepdims=True)
        h_ref[:, h * ML_V:(h + 1) * ML_V] = num / jnp.maximum(jnp.abs(den), jnp.exp(-mj))
        ws = g11 - a_col + i_col
        m_new = jnp.maximum(g11 + m11, jnp.max(ws, axis=0, keepdims=True))
        dec = jnp.exp(g11 + m11 - m_new)
        ek = jnp.exp(ws - m_new) * kf
        c_ref[h] = dec * cmat + lax.dot_general(ek.astype(bf16), vb, (((0,), (0,)), ((), ())),
                                                preferred_element_type=f32)
        n_ref[h] = dec * nvec + jnp.sum(ek, axis=0, keepdims=True)
        m_ref[h] = jnp.broadcast_to(m_new, (1, LANES))


def mlstm_scan(proj, gcol, grow, lay, lc, backward, q_off, k_off, v_off):
    t = proj.shape[0]
    nchunks = t // lc
    hq, hv = ML_HEADS * ML_QK, ML_W
    cidx = (lambda c: nchunks - 1 - c) if backward else (lambda c: c)
    return pl.pallas_call(
        functools.partial(_mlstm_body, lc=lc, lay=lay, backward=backward),
        grid=(nchunks,),
        in_specs=[pl.BlockSpec((lc, hq), lambda c: (cidx(c), q_off // hq)),
                  pl.BlockSpec((lc, hq), lambda c: (cidx(c), k_off // hq)),
                  pl.BlockSpec((lc, hv), lambda c: (cidx(c), v_off // hv)),
                  pl.BlockSpec((lc, LANES), lambda c: (cidx(c), 0)),
                  pl.BlockSpec((grow.shape[0], lc), lambda c: (0, cidx(c)))],
        out_specs=pl.BlockSpec((lc, hv), lambda c: (cidx(c), 0)),
        out_shape=jax.ShapeDtypeStruct((t, hv), f32),
        scratch_shapes=[pltpu.VMEM((ML_HEADS, ML_QK, ML_V), f32), pltpu.VMEM((ML_HEADS, 1, ML_QK), f32),
                        pltpu.VMEM((ML_HEADS, 1, LANES), f32)],
        compiler_params=_cparams("arbitrary"),
        name="mlstm_bwd" if backward else "mlstm_fwd",
    )(proj, proj, proj, gcol, grow)


def _mlstm_post_body(hf_ref, hb_ref, o_ref, g_ref, y_ref):
    hsum = hf_ref[...] + hb_ref[...]
    ms = jnp.mean(hsum * hsum, axis=-1, keepdims=True)
    y_ref[...] = (jax.nn.sigmoid(o_ref[...]) * (hsum * lax.rsqrt(ms + EPS) * g_ref[...])).astype(y_ref.dtype)


def mlstm_post(hf, hb, proj, o_off, ml_g, l, tm):
    t = hf.shape[0]
    blk = lambda off: pl.BlockSpec((tm, ML_V), lambda i, h: (i, off + h))
    return pl.pallas_call(
        _mlstm_post_body,
        grid=(t // tm, ML_HEADS),
        in_specs=[blk(0), blk(0), blk(o_off // ML_V), pl.BlockSpec((None, 1, ML_V), lambda i, h: (l, 0, h))],
        out_specs=blk(0),
        out_shape=jax.ShapeDtypeStruct((t, ML_W), bf16),
        compiler_params=_cparams("parallel", "parallel"),
        name="mlstm_post",
    )(hf, hb, proj, ml_g.reshape(DEPTH, 1, ML_W))


def mlstm_gate_tables(gates, gate_b, lc):
    t = gates.shape[0]
    hd = ML_HEADS
    g = gates[:, :4 * hd].reshape(t, 4, hd) + gate_b.astype(f32)
    i_f, f_f, i_b, f_b = g[:, 0], g[:, 1], g[:, 2], g[:, 3]
    lf_f = jax.nn.log_sigmoid(f_f).reshape(t // lc, lc, hd)
    lf_b = jax.nn.log_sigmoid(f_b).reshape(t // lc, lc, hd)
    a_f = jnp.cumsum(lf_f, axis=1)
    a_b = jnp.flip(jnp.cumsum(jnp.flip(lf_b, axis=1), axis=1), axis=1)
    tot_f = jnp.broadcast_to(a_f[:, -1:, :], a_f.shape)
    tot_b = jnp.broadcast_to(a_b[:, :1, :], a_b.shape)
    flat = lambda a: a.reshape(t, hd)
    cols = jnp.concatenate([flat(a_f), flat(a_b), i_f, i_b, flat(tot_f), flat(tot_b)], axis=1)
    gcol = jnp.pad(cols, ((0, 0), (0, LANES - cols.shape[1])))
    grow = jnp.pad(cols.T, ((0, (-cols.shape[1]) % 8), (0, 0)))
    return gcol, grow


def _hyconv_body(c_ref, p_ref, n_ref, w_ref, b_ref, u_ref, ub_ref, *, tm, lay):
    r0 = pl.program_id(0) * tm
    x = c_ref[...]
    prev_row = jnp.where(lay.is_boundary(r0), 0.0, p_ref[7:8, :])
    next_row = jnp.where(lay.is_boundary(r0 + tm), 0.0, n_ref[0:1, :])
    rid = lax.broadcasted_iota(jnp.int32, x.shape, 0)
    xm = jnp.where(rid == 0, prev_row, pltpu.roll(x, 1, axis=0))
    xp = jnp.where(rid == tm - 1, next_row, pltpu.roll(x, tm - 1, axis=0))
    u = xm * w_ref[0:1, :] + x * w_ref[1:2, :] + xp * w_ref[2:3, :] + b_ref[...]
    u_ref[...] = u
    ub_ref[...] = u.astype(bf16)


def hyena_short_conv(proj, hy_off, conv_w, conv_b, l, lay, tm, tc):
    t = proj.shape[0]
    width = 3 * HY_W
    c0 = hy_off // tc
    nblk8 = t // 8
    return pl.pallas_call(
        functools.partial(_hyconv_body, tm=tm, lay=lay),
        grid=(t // tm, width // tc),
        in_specs=[pl.BlockSpec((tm, tc), lambda i, j: (i, c0 + j)),
                  pl.BlockSpec((8, tc), lambda i, j: (jnp.maximum(i * (tm // 8) - 1, 0), c0 + j)),
                  pl.BlockSpec((8, tc), lambda i, j: (jnp.minimum((i + 1) * (tm // 8), nblk8 - 1), c0 + j)),
                  pl.BlockSpec((None, 3, tc), lambda i, j: (l, 0, j)),
                  pl.BlockSpec((None, 1, tc), lambda i, j: (l, 0, j))],
        out_specs=[pl.BlockSpec((tm, tc), lambda i, j: (i, j))] * 2,
        out_shape=[jax.ShapeDtypeStruct((t, width), f32), jax.ShapeDtypeStruct((t, width), bf16)],
        compiler_params=_cparams("parallel", "parallel"),
        name="hyena_short_conv",
    )(proj, proj, proj, conv_w, conv_b.reshape(DEPTH, 1, width))


def hyena_time_filters(seq_len, w1, b1, freq, w2, b2, w3, b3):
    t = jnp.linspace(0.0, 1.0, seq_len, dtype=f32)[:, None]
    bands = (HY_EMB - 1) // 2
    fr = jnp.linspace(1e-4, bands - 1, bands, dtype=f32)
    ang = (2.0 * math.pi / seq_len) * jnp.arange(seq_len, dtype=f32)[:, None] * fr[None, :]
    z = jnp.concatenate([t, jnp.cos(ang), -jnp.sin(ang)], axis=-1)
    fq = freq.astype(f32)
    h = jnp.sin(fq[0] * (jnp.dot(z, w1, precision=HI) + b1))
    h = jnp.sin(fq[1] * (jnp.dot(h, w2, precision=HI) + b2))
    h = (jnp.dot(h, w3, precision=HI) + b3).reshape(seq_len, HY_ORDER, 2, HY_W)
    deltas = jnp.linspace(math.log(HY_TARGET) / HY_SLOW, math.log(HY_TARGET) / HY_FAST, HY_W, dtype=f32)
    h = h * jnp.exp(-t * jnp.abs(deltas))[:, None, None, :]
    hf, hb = h[:, :, 0], h[:, :, 1]
    first = (jnp.arange(seq_len) == 0)[:, None, None]
    hf = jnp.where(first, hf + hb, hf)
    hb = jnp.where(first, 0.0, hb)
    p = (hf + hb).reshape(seq_len, HY_ORDER * HY_W)
    q = (hf - hb).reshape(seq_len, HY_ORDER * HY_W)
    return p.astype(bf16), q.astype(bf16)


def dft_matrices(seq_len):
    k = jnp.arange(seq_len, dtype=jnp.int32)[:, None]

    def table(n):
        ang = ((k * n[None, :]) % (2 * seq_len)).astype(f32) * (math.pi / seq_len)
        return jnp.cos(ang), jnp.sin(ang)

    ca, sa = table(jnp.arange(seq_len // LANES, dtype=jnp.int32) * LANES)
    cb, sb = table(jnp.arange(LANES, dtype=jnp.int32))
    ca, sa, cb, sb = ca[:, :, None], sa[:, :, None], cb[:, None, :], sb[:, None, :]
    cos_m = (ca * cb - sa * sb).reshape(seq_len, seq_len).astype(bf16)
    nsin_m = (-(sa * cb + ca * sb)).reshape(seq_len, seq_len).astype(bf16)
    alt = jnp.where(jnp.arange(seq_len) % 2 == 0, 1.0, -1.0).astype(bf16)
    alt8 = jnp.zeros((8, seq_len), bf16).at[0].set(alt)
    return cos_m, nsin_m, alt8


def _hy_filter_body(fc_ref, fs_ref, alt_ref, p_ref, q_ref, hr_ref, hi_ref):
    p = p_ref[...]
    hr_ref[...] = jnp.dot(fc_ref[...], p, preferred_element_type=f32)
    hi = jnp.dot(fs_ref[...], q_ref[...], preferred_element_type=f32)
    nyq = jnp.dot(alt_ref[...], p, preferred_element_type=f32)[0:1, :]
    grow = pl.program_id(0) * hr_ref.shape[0] + lax.broadcasted_iota(jnp.int32, hi.shape, 0)
    hi_ref[...] = jnp.where(grow == 0, nyq, hi)


def hyena_filter_spectrum(cos_m, nsin_m, alt8, p, q, tm, tn):
    seq_len, n = p.shape
    a_spec = pl.BlockSpec((tm, seq_len), lambda i, j: (i, 0))
    b_spec = pl.BlockSpec((seq_len, tn), lambda i, j: (0, j))
    o_spec = pl.BlockSpec((tm, tn), lambda i, j: (i, j))
    shp = jax.ShapeDtypeStruct((seq_len, n), f32)
    return pl.pallas_call(
        _hy_filter_body,
        grid=(seq_len // tm, n // tn),
        in_specs=[a_spec, a_spec, pl.BlockSpec((8, seq_len), lambda i, j: (0, 0)), b_spec, b_spec],
        out_specs=[o_spec, o_spec], out_shape=[shp, shp],
        compiler_params=_cparams("parallel", "parallel"),
        name="hyena_filter_spectrum",
    )(cos_m, nsin_m, alt8, p, q)


def _hy_fwd_body(fc_ref, fs_ref, alt_ref, z_ref, hr_ref, hi_ref, yr_ref, yi_ref, *, n_fft):
    z = z_ref[...]
    vr = jnp.dot(fc_ref[...], z, preferred_element_type=f32)
    vi = jnp.dot(fs_ref[...], z, preferred_element_type=f32)
    nyq = jnp.dot(alt_ref[...], z, preferred_element_type=f32)[0:1, :]
    hr, hi = hr_ref[...], hi_ref[...]
    grow = pl.program_id(0) * vr.shape[0] + lax.broadcasted_iota(jnp.int32, vr.shape, 0)
    row0 = grow == 0
    vihi = jnp.where(row0, nyq, vi) * hi
    yr = vr * hr - jnp.where(row0, 0.0, vihi)
    yi = jnp.where(row0, vihi, vr * hi + vi * hr)
    scale = jnp.where(row0, 1.0 / n_fft, 2.0 / n_fft)
    yr_ref[...] = (yr * scale).astype(bf16)
    yi_ref[...] = (yi * scale).astype(bf16)


def hyena_fwd(cos_m, nsin_m, alt8, zb, z_col, hr, hi, order, row_start, seq_len, nseq, tm, tn):
    tm, tn = min(tm, seq_len), min(tn, HY_W)
    rows = nseq * seq_len
    s0, i0 = row_start // seq_len, seq_len // tm
    a_spec = pl.BlockSpec((tm, seq_len), lambda i, s, j: (i, 0))
    h_spec = pl.BlockSpec((tm, tn), lambda i, s, j: (i, order * (HY_W // tn) + j))
    o_spec = pl.BlockSpec((tm, tn), lambda i, s, j: (s * i0 + i, j))
    shp = jax.ShapeDtypeStruct((rows, HY_W), bf16)
    return pl.pallas_call(
        functools.partial(_hy_fwd_body, n_fft=2 * seq_len),
        grid=(seq_len // tm, nseq, HY_W // tn),
        in_specs=[a_spec, a_spec, pl.BlockSpec((8, seq_len), lambda i, s, j: (0, 0)),
                  pl.BlockSpec((seq_len, tn), lambda i, s, j: (s0 + s, z_col // tn + j)),
                  h_spec, h_spec],
        out_specs=[o_spec, o_spec], out_shape=[shp, shp],
        compiler_params=_cparams("parallel", "parallel", "parallel"),
        name="hyena_fwd_dft",
    )(cos_m, nsin_m, alt8, zb, hr, hi)


def _hy_inv_body(fc_ref, ft_ref, yr_ref, yi_ref, z_ref, x_ref, d_ref, *rest, last):
    y = (jnp.dot(fc_ref[...], yr_ref[...], preferred_element_type=f32)
         + jnp.dot(ft_ref[...], yi_ref[...], preferred_element_type=f32))
    n = pl.program_id(0) * y.shape[0] + lax.broadcasted_iota(jnp.int32, (y.shape[0], 1), 0)
    y = y + jnp.where(n % 2 == 0, 1.0, -1.0) * yi_ref[0:1, :].astype(f32)
    z = x_ref[...] * (y + d_ref[...] * z_ref[...])
    if last:
        g_ref, o_ref = rest
        gw = HY_W // HY_GROUPS
        for gi in range(z.shape[1] // gw):
            seg = z[:, gi * gw:(gi + 1) * gw]
            ms = jnp.mean(seg * seg, axis=-1, keepdims=True)
            o_ref[:, gi * gw:(gi + 1) * gw] = (seg * lax.rsqrt(ms + EPS)
                                               * g_ref[:, gi * gw:(gi + 1) * gw]).astype(o_ref.dtype)
    else:
        o_ref, ob_ref = rest
        o_ref[...] = z
        ob_ref[...] = z.astype(bf16)


def hyena_inv(cos_m, nsin_m, yr, yi, z_arr, z_row0, z_col, u, x_col, d_bias, gain, l, order,
              row_start, seq_len, nseq, tm, tn):
    tm, tn = min(tm, seq_len), min(tn, HY_W)
    last = order == HY_ORDER - 1
    rows = nseq * seq_len
    i0 = seq_len // tm
    a_spec = pl.BlockSpec((tm, seq_len), lambda i, s, j: (i, 0))
    b_spec = pl.BlockSpec((seq_len, tn), lambda i, s, j: (s, j))
    e_spec = lambda row0, col: pl.BlockSpec((tm, tn), lambda i, s, j: (row0 // tm + s * i0 + i, col // tn + j))
    o_spec = pl.BlockSpec((tm, tn), lambda i, s, j: (s * i0 + i, j))
    in_specs = [a_spec, a_spec, b_spec, b_spec, e_spec(z_row0, z_col), e_spec(row_start, x_col),
                pl.BlockSpec((None, 1, tn), lambda i, s, j: (l * HY_ORDER + order, 0, j))]
    args = [cos_m, nsin_m, yr, yi, z_arr, u, d_bias.reshape(DEPTH * HY_ORDER, 1, HY_W)]
    if last:
        in_specs.append(pl.BlockSpec((None, 1, tn), lambda i, s, j: (l, 0, j)))
        args.append(gain.reshape(DEPTH, 1, HY_W))
        out_specs = o_spec
        out_shape = jax.ShapeDtypeStruct((rows, HY_W), bf16)
    else:
        out_specs = [o_spec, o_spec]
        out_shape = [jax.ShapeDtypeStruct((rows, HY_W), f32), jax.ShapeDtypeStruct((rows, HY_W), bf16)]
    return pl.pallas_call(
        functools.partial(_hy_inv_body, last=last),
        grid=(seq_len // tm, nseq, HY_W // tn),
        in_specs=in_specs, out_specs=out_specs, out_shape=out_shape,
        compiler_params=_cparams("parallel", "parallel", "parallel"),
        name="hyena_inv_dft",
    )(*args)


def hyena_group(u, ub, spec, d_bias, gain, l, row_start, seq_len, nseq):
    cos_m, nsin_m, alt8, hr, hi = spec
    grp = (row_start, seq_len, nseq, TM_DFT, TN_DFT)
    yr, yi = hyena_fwd(cos_m, nsin_m, alt8, ub, 0, hr, hi, 0, *grp)
    z1, z1b = hyena_inv(cos_m, nsin_m, yr, yi, u, row_start, 0, u, HY_W, d_bias, gain, l, 0, *grp)
    yr, yi = hyena_fwd(cos_m, nsin_m, alt8, z1b, 0, hr, hi, 1, 0, seq_len, nseq, TM_DFT, TN_DFT)
    return hyena_inv(cos_m, nsin_m, yr, yi, z1, 0, 0, u, 2 * HY_W, d_bias, gain, l, 1, *grp)


def _item_changed(w, e_ref, c_ref):
    prev = jnp.maximum(w - 1, 0)
    return (w == 0) | (e_ref[w] != e_ref[prev]) | (c_ref[w] != c_ref[prev])


def _ffn_up_body(ib_ref, ic_ref, ie_ref, n_ref, x_ref, wg_ref, wu_ref, h_ref, wgb_ref, wub_ref):
    w = pl.program_id(0)

    @pl.when(w < n_ref[0])
    def _():
        @pl.when(_item_changed(w, ie_ref, ic_ref))
        def _():
            wgb_ref[...] = wg_ref[...].astype(bf16)
            wub_ref[...] = wu_ref[...].astype(bf16)

        x = x_ref[...]
        g = jnp.dot(x, wgb_ref[...], preferred_element_type=f32)
        u = jnp.dot(x, wub_ref[...], preferred_element_type=f32)
        h_ref[...] = (g * jax.nn.sigmoid(g) * u).astype(h_ref.dtype)


def _ffn_down_body(ib_ref, ic_ref, ie_ref, n_ref, h_ref, wd_ref, y_ref, wdb_ref):
    w = pl.program_id(0)

    @pl.when(w < n_ref[0])
    def _():
        @pl.when(_item_changed(w, ie_ref, ic_ref))
        def _():
            wdb_ref[...] = wd_ref[...].astype(bf16)

        y_ref[...] = jnp.dot(h_ref[...], wdb_ref[...], preferred_element_type=f32).astype(y_ref.dtype)


def ffn_items(plan, nb, ncol):
    blocks_per_e, first_block_e, n_used = plan
    n_items = n_used[0] * ncol
    w = jnp.minimum(jnp.arange(nb * ncol, dtype=jnp.int32), n_items - 1)
    istart = first_block_e * ncol
    e = jnp.sum((istart[None, :] <= w[:, None]).astype(jnp.int32), axis=1) - 1
    local = w - istart[e]
    cnt = jnp.maximum(blocks_per_e[e], 1)
    item_c = (local // cnt).astype(jnp.int32)
    item_b = (first_block_e[e] + local % cnt).astype(jnp.int32)
    return item_b, item_c, e, n_items[None].astype(jnp.int32)


def grouped_ffn(x_rows, plan, wg, wu, wd, l, tb, tf, tn, out_dtype):
    rows, d = x_rows.shape
    fdim = wg.shape[3]
    nb = rows // tb
    h = pl.pallas_call(
        _ffn_up_body,
        grid_spec=pltpu.PrefetchScalarGridSpec(
            num_scalar_prefetch=4, grid=(nb * (fdim // tf),),
            in_specs=[pl.BlockSpec((tb, d), lambda w, ib, ic, ie, n: (ib[w], 0)),
                      pl.BlockSpec((None, None, d, tf), lambda w, ib, ic, ie, n: (l, ie[w], 0, ic[w])),
                      pl.BlockSpec((None, None, d, tf), lambda w, ib, ic, ie, n: (l, ie[w], 0, ic[w]))],
            out_specs=pl.BlockSpec((tb, tf), lambda w, ib, ic, ie, n: (ib[w], ic[w])),
            scratch_shapes=[pltpu.VMEM((d, tf), bf16), pltpu.VMEM((d, tf), bf16)]),
        out_shape=jax.ShapeDtypeStruct((rows, fdim), bf16),
        compiler_params=_cparams("arbitrary"),
        name="ffn_up",
    )(*ffn_items(plan, nb, fdim // tf), x_rows, wg, wu)
    return pl.pallas_call(
        _ffn_down_body,
        grid_spec=pltpu.PrefetchScalarGridSpec(
            num_scalar_prefetch=4, grid=(nb * (d // tn),),
            in_specs=[pl.BlockSpec((tb, fdim), lambda w, ib, ic, ie, n: (ib[w], 0)),
                      pl.BlockSpec((None, None, fdim, tn), lambda w, ib, ic, ie, n: (l, ie[w], 0, ic[w]))],
            out_specs=pl.BlockSpec((tb, tn), lambda w, ib, ic, ie, n: (ib[w], ic[w])),
            scratch_shapes=[pltpu.VMEM((fdim, tn), bf16)]),
        out_shape=jax.ShapeDtypeStruct((rows, d), out_dtype),
        compiler_params=_cparams("arbitrary"),
        name="ffn_down",
    )(*ffn_items(plan, nb, d // tn), h, wd)


def route_tokens(logits, router_b):
    t = logits.shape[0]
    per = N_EXPERTS // N_GROUPS
    s = jax.nn.sigmoid(logits)
    sb = s + router_b.astype(f32)

    def pop_max(x):
        m = jnp.max(x, axis=-1, keepdims=True)
        iota = lax.broadcasted_iota(jnp.int32, x.shape, x.ndim - 1)
        first = jnp.min(jnp.where(x == m, iota, x.shape[-1]), axis=-1, keepdims=True)
        hit = iota == first
        return m, first, hit, jnp.where(hit, -jnp.inf, x)

    m1, _, _, rest = pop_max(sb.reshape(t, N_GROUPS, per))
    gscore = (m1 + jnp.max(rest, axis=-1, keepdims=True))[..., 0]
    gmask = jnp.zeros(gscore.shape, bool)
    for _ in range(TOPK_GROUPS):
        _, _, hit, gscore = pop_max(gscore)
        gmask = gmask | hit
    cur = jnp.where(jnp.repeat(gmask, per, axis=1), sb, -jnp.inf)
    idx, gate = [], []
    for _ in range(TOP_K):
        _, first, hit, cur = pop_max(cur)
        idx.append(first)
        gate.append(jnp.sum(jnp.where(hit, s, 0.0), axis=-1, keepdims=True))
    idx = jnp.concatenate(idx, axis=1)
    gate = jnp.concatenate(gate, axis=1)
    gate = gate / jnp.sum(gate, axis=-1, keepdims=True) * ROUTED_SCALE
    return idx, gate


def _row_gather_body(nused_ref, tok_ref, src_ref, out_ref, buf_ref, sem):
    tb = out_ref.shape[0]

    @pl.when(pl.program_id(0) < nused_ref[0])
    def _():
        def issue(r, carry):
            pltpu.make_async_copy(src_ref.at[pl.ds(tok_ref[0, r], 1)], buf_ref.at[pl.ds(r, 1)], sem).start()
            return carry

        lax.fori_loop(0, tb, issue, 0, unroll=8)
        pltpu.make_async_copy(src_ref.at[pl.ds(0, tb)], buf_ref, sem).wait()
        out_ref[...] = buf_ref[...].astype(out_ref.dtype)


def gather_rows(src, row_tok, n_used, tb):
    rows = row_tok.shape[0]
    d = src.shape[1]
    nb = rows // tb
    return pl.pallas_call(
        _row_gather_body,
        grid_spec=pltpu.PrefetchScalarGridSpec(
            num_scalar_prefetch=1, grid=(nb,),
            in_specs=[pl.BlockSpec((None, 1, tb), lambda b, n: (b, 0, 0), memory_space=pltpu.SMEM),
                      pl.BlockSpec(memory_space=pl.ANY)],
            out_specs=pl.BlockSpec((tb, d), lambda b, n: (b, 0)),
            scratch_shapes=[pltpu.VMEM((tb, d), src.dtype), pltpu.SemaphoreType.DMA(())]),
        out_shape=jax.ShapeDtypeStruct((rows, d), bf16),
        compiler_params=_cparams("arbitrary"),
        name="gather_rows",
    )(n_used, row_tok.reshape(nb, 1, tb), src)


def dispatch_plan(idx, tb):
    t = idx.shape[0]
    n = t * TOP_K
    e_flat = idx.reshape(n).astype(jnp.int32)
    onehot = (e_flat[:, None] == jnp.arange(N_EXPERTS, dtype=jnp.int32)[None, :]).astype(jnp.int32)
    csum = jnp.cumsum(onehot, axis=0)
    counts = csum[-1]
    rank = jnp.take_along_axis(csum, e_flat[:, None], axis=1)[:, 0] - 1
    padded = (counts + tb - 1) // tb * tb
    pends = jnp.cumsum(padded)
    pstarts = pends - padded
    pos = pstarts[e_flat] + rank
    nb = (n + N_EXPERTS * (tb - 1) + tb - 1) // tb
    tok = jnp.arange(n, dtype=jnp.int32) // TOP_K
    row_tok = jnp.zeros((nb * tb,), jnp.int32).at[pos].set(tok, unique_indices=True)
    plan = ((padded // tb).astype(jnp.int32), (pstarts // tb).astype(jnp.int32),
            (pends[-1:] // tb).astype(jnp.int32))
    return row_tok, pos.reshape(t, TOP_K), plan


def _final_body(x_ref, moe_ref, sh_ref, mod_ref, o_ref, *, gate_idx):
    o_ref[...] = x_ref[...] + mod_ref[gate_idx:gate_idx + 1, :] * (moe_ref[...] + sh_ref[...])


def gated_residual_sum(x, moe_y, shared_y, mod, l, lay, gate_idx, tm):
    t, d = x.shape
    seq = lambda i: lay.seq_of_row(i * tm)
    blk = pl.BlockSpec((tm, d), lambda i: (i, 0))
    return pl.pallas_call(
        functools.partial(_final_body, gate_idx=gate_idx),
        grid=(t // tm,),
        in_specs=[blk, blk, blk, pl.BlockSpec((None, None, N_MOD, d), lambda i: (l, seq(i), 0, 0))],
        out_specs=blk, out_shape=jax.ShapeDtypeStruct((t, d), f32),
        compiler_params=_cparams("parallel"),
        name="gated_residual_sum",
    )(x, moe_y, shared_y, mod)


def _col_offsets():
    sizes = [ATT_W, ATT_W, ATT_W, 3 * HY_W, ML_HEADS * ML_QK, ML_HEADS * ML_QK, ML_W, ML_W]
    return np.concatenate([[0], np.cumsum(sizes)]).tolist()


def kernel(x_prompt, x_sample, c_prompt, c_sample, ada_w, ada_b, norm1_g, norm2_g, w_in, w_out, qk_norm_g,
           diff_lambda, attn_out_g, hy_conv_w, hy_conv_b, hy_f_w1, hy_f_b1, hy_f_freq, hy_f_w2, hy_f_b2,
           hy_f_w3, hy_f_b3, hy_bias, hy_out_g, ml_gate_b, ml_out_g, router_w, router_b, exp_w_gate,
           exp_w_up, exp_w_down, sh_w_gate, sh_w_up, sh_w_down):
    nb1, s1, d = x_prompt.shape
    nb2, s2, _ = x_sample.shape
    lay = _Layout(nb1, s1, nb2, s2)
    t = lay.t
    offs = _col_offsets()
    n_main = offs[-1]
    q_off, k_off, v_off, hy_off, mq_off, mk_off, mv_off, mo_off = offs[:8]
    assert HY_W // HY_GROUPS == LANES and ATT_V == LANES

    x = jnp.concatenate([x_prompt.reshape(nb1 * s1, d), x_sample.reshape(nb2 * s2, d)], axis=0)
    c_all = jnp.concatenate([c_prompt, c_sample], axis=0).astype(f32)
    c_pad = jnp.pad(c_all, ((0, (-lay.nseq) % 8), (0, 0)))
    mod = ada_modulation(c_pad, ada_w, ada_b, tn=min(1024, d))
    mod = mod.reshape(DEPTH, c_pad.shape[0], N_MOD, d)

    w_in_b = w_in.astype(bf16)
    w_gate_b = jnp.pad(w_in[:, :, n_main:], ((0, 0), (0, 0), (0, LANES - (w_in.shape[2] - n_main)))).astype(bf16)
    w_out_b = w_out.astype(bf16)
    tf_moe, tn_moe = min(TF_MOE, exp_w_gate.shape[3]), min(TN_MOE, d)
    dft = {sl: dft_matrices(sl) for sl in sorted({s1, s2})}
    slopes, slope_lanes = alibi_slope_lanes()

    tm_mm = math.gcd(TM_MM, s2)
    for l in range(DEPTH):
        lam_init = 0.8 - 0.6 * math.exp(-0.3 * l)
        h1 = norm_modulate(x, mod, norm1_g, l, lay, 0, 1)[0]
        proj = matmul(h1, w_in_b, l, n_main, tm_mm, TN_IN)
        gates = matmul(h1, w_gate_b, l, LANES, tm_mm, LANES)

        lp = diff_lambda[l].astype(f32)
        lam = jnp.exp(jnp.sum(lp[0] * lp[1])) - jnp.exp(jnp.sum(lp[2] * lp[3])) + lam_init
        scal = jnp.concatenate([slopes, lam[None], jnp.zeros((3,), f32)])
        qn, kn, vb = attn_prepare(proj, qk_norm_g[l].reshape(2, ATT_V).astype(f32), lay, tm_mm)
        out_g = attn_out_g[l].reshape(1, ATT_V).astype(f32)
        ya = jnp.concatenate([
            attention_group(scal, slope_lanes, qn, kn, vb, out_g, rs, sl, ns, TQ_ATT, TK_ATT, TW_ATT, 1.0 - lam_init)
            for rs, sl, ns in lay.groups()], axis=0)

        u, ub = hyena_short_conv(proj, hy_off, hy_conv_w, hy_conv_b, l, lay, TM_ROW, math.gcd(hy_off, HY_W, 512))
        yb_parts = []
        for rs, sl, ns in lay.groups():
            cos_m, nsin_m, alt8 = dft[sl]
            p, q = hyena_time_filters(sl, hy_f_w1[l], hy_f_b1[l], hy_f_freq[l], hy_f_w2[l], hy_f_b2[l],
                                      hy_f_w3[l], hy_f_b3[l])
            hr, hi = hyena_filter_spectrum(cos_m, nsin_m, alt8, p, q, min(TM_DFT, sl), TN_DFT)
            yb_parts.append(hyena_group(u, ub, (cos_m, nsin_m, alt8, hr, hi), hy_bias, hy_out_g, l, rs, sl, ns))
        yb = jnp.concatenate(yb_parts, axis=0)

        gcol, grow = mlstm_gate_tables(gates, ml_gate_b[l], ML_CHUNK)
        hf = mlstm_scan(proj, gcol, grow, lay, ML_CHUNK, False, mq_off, mk_off, mv_off)
        hb = mlstm_scan(proj, gcol, grow, lay, ML_CHUNK, True, mq_off, mk_off, mv_off)
        yc = mlstm_post(hf, hb, proj, mo_off, ml_out_g, l, tm_mm)

        ycat = jnp.concatenate([ya, yb, yc], axis=1)
        x = matmul_gated_residual(ycat, w_out_b, l, x, mod, lay, 2, tm_mm, TN_OUT)

        h2, h2f, logits = norm_modulate(x, mod, norm2_g, l, lay, 3, 4, router_w)
        routed = [route_tokens(logits[rs:rs + sl * ns], router_b[l]) for rs, sl, ns in lay.groups()]
        idx = jnp.concatenate([r[0] for r in routed], axis=0)
        gate = jnp.concatenate([r[1] for r in routed], axis=0)
        row_tok, pos, plan = dispatch_plan(idx, TB_MOE)
        x_rows = gather_rows(h2f, row_tok, plan[2], TB_MOE)
        tb_sh = math.gcd(TB_MOE, t)
        one = jnp.full((1,), t // tb_sh, jnp.int32)
        shared_y = grouped_ffn(h2, (one, jnp.zeros((1,), jnp.int32), one), sh_w_gate[:, None], sh_w_up[:, None],
                               sh_w_down[:, None], l, tb_sh, tf_moe, tn_moe, bf16)
        y_rows = grouped_ffn(x_rows, plan, exp_w_gate, exp_w_up, exp_w_down, l, TB_MOE, tf_moe, tn_moe, bf16)
        moe_y = jnp.sum(jnp.take(y_rows, pos, axis=0).astype(f32) * gate[:, :, None], axis=1)
        x = gated_residual_sum(x, moe_y, shared_y, mod, l, lay, 5, TM_ROW)

    return (x[:lay.p].reshape(nb1, s1, d), x[lay.p:].reshape(nb2, s2, d))
```

```python
import functools
import math

import jax
import jax.numpy as jnp
import numpy as np
from jax import lax
from jax.experimental import pallas as pl
from jax.experimental.pallas import tpu as pltpu

DEPTH = 2
ATT_QK = 64
ATT_V = 2 * ATT_QK
ATT_HEADS = 12
ATT_W = ATT_HEADS * ATT_V
HY_W = 1024
HY_ORDER = 2
HY_GROUPS = 8
HY_EMB = 33
HY_TARGET = 1e-2
HY_FAST = 0.3
HY_SLOW = 1.5
ML_V = 256
ML_QK = ML_V // 2
ML_HEADS = 6
ML_W = ML_HEADS * ML_V
N_EXPERTS = 64
TOP_K = 8
N_GROUPS = 8
TOPK_GROUPS = 4
ROUTED_SCALE = 2.5
N_MOD = 6
EPS = 1e-6

LANES = 128
VMEM_LIMIT = 56 * 1024 * 1024

ML_CHUNK = 256
TB_MOE = 512
TF_MOE = 512
TN_MOE = 2048
TM_MM = 1024
TN_IN = 768
TN_OUT = 512
TM_ROW = 256
TQ_ATT = 1024
TK_ATT = 512
TW_ATT = 2048
TM_DFT = 512
TN_DFT = 256
DFT_SHORT = 2048
TM_CONV = 1024

f32 = jnp.float32
bf16 = jnp.bfloat16
HI = lax.Precision.HIGHEST


def _cparams(*sem):
    return pltpu.CompilerParams(dimension_semantics=sem, vmem_limit_bytes=VMEM_LIMIT)


class _Layout:
    def __init__(self, nb1, s1, nb2, s2):
        self.nb1, self.s1, self.nb2, self.s2 = nb1, s1, nb2, s2
        self.p = nb1 * s1
        self.t = self.p + nb2 * s2
        self.nseq = nb1 + nb2

    def seq_of_row(self, r):
        return jnp.where(r < self.p, r // self.s1, self.nb1 + (r - self.p) // self.s2)

    def is_boundary(self, r):
        return jnp.where(r <= self.p, r % self.s1 == 0, (r - self.p) % self.s2 == 0)

    def groups(self):
        return ((0, self.s1, self.nb1), (self.p, self.s2, self.nb2))


def _ada_body(c_ref, w_ref, b_ref, o_ref):
    c = c_ref[...]
    s = c * jax.nn.sigmoid(c)
    o_ref[...] = jnp.dot(s, w_ref[...], preferred_element_type=f32, precision=HI) + b_ref[...]


def ada_modulation(c_pad, ada_w, ada_b, tn):
    r, d = c_pad.shape
    n = ada_w.shape[2]
    return pl.pallas_call(
        _ada_body,
        grid=(DEPTH, n // tn),
        in_specs=[pl.BlockSpec((r, d), lambda l, j: (0, 0)),
                  pl.BlockSpec((None, d, tn), lambda l, j: (l, 0, j)),
                  pl.BlockSpec((None, 1, tn), lambda l, j: (l, 0, j))],
        out_specs=pl.BlockSpec((None, r, tn), lambda l, j: (l, 0, j)),
        out_shape=jax.ShapeDtypeStruct((DEPTH, r, n), f32),
        compiler_params=_cparams("parallel", "parallel"),
        name="ada_modulation",
    )(c_pad, ada_w, ada_b.reshape(DEPTH, 1, n))


def _norm_mod_body(x_ref, mod_ref, g_ref, *rest, shift_idx, scale_idx, with_router):
    x = x_ref[...]
    ms = jnp.mean(x * x, axis=-1, keepdims=True)
    y = x * lax.rsqrt(ms + EPS) * g_ref[...]
    h = y * (1.0 + mod_ref[scale_idx:scale_idx + 1, :]) + mod_ref[shift_idx:shift_idx + 1, :]
    if with_router:
        rw_ref, h_ref, hf_ref, logit_ref = rest
        logit_ref[...] = jnp.dot(h, rw_ref[...], preferred_element_type=f32, precision=HI)
        hf_ref[...] = h
    else:
        (h_ref,) = rest
    h_ref[...] = h.astype(bf16)


def norm_modulate(x, mod, gain, l, lay, shift_idx, scale_idx, router_w=None):
    t, d = x.shape
    tm = TM_ROW
    seq = lambda i: lay.seq_of_row(i * tm)
    in_specs = [pl.BlockSpec((tm, d), lambda i: (i, 0)),
                pl.BlockSpec((None, None, N_MOD, d), lambda i: (l, seq(i), 0, 0)),
                pl.BlockSpec((None, 1, d), lambda i: (l, 0, 0))]
    args = [x, mod, gain.reshape(DEPTH, 1, d)]
    out_specs = [pl.BlockSpec((tm, d), lambda i: (i, 0))]
    out_shape = [jax.ShapeDtypeStruct((t, d), bf16)]
    if router_w is not None:
        e = router_w.shape[2]
        in_specs.append(pl.BlockSpec((None, d, e), lambda i: (l, 0, 0)))
        args.append(router_w)
        out_specs += [pl.BlockSpec((tm, d), lambda i: (i, 0)), pl.BlockSpec((tm, e), lambda i: (i, 0))]
        out_shape += [jax.ShapeDtypeStruct((t, d), f32), jax.ShapeDtypeStruct((t, e), f32)]
    return pl.pallas_call(
        functools.partial(_norm_mod_body, shift_idx=shift_idx, scale_idx=scale_idx,
                          with_router=router_w is not None),
        grid=(t // tm,), in_specs=in_specs, out_specs=out_specs, out_shape=out_shape,
        compiler_params=_cparams("parallel"),
        name="norm_modulate",
    )(*args)


def _mm_body(a_ref, b_ref, o_ref):
    o_ref[...] = jnp.dot(a_ref[...], b_ref[...], preferred_element_type=f32).astype(o_ref.dtype)


def matmul(a, b, l, n_out, tm, tn, out_dtype=f32):
    m, k = a.shape
    return pl.pallas_call(
        _mm_body,
        grid=(m // tm, n_out // tn),
        in_specs=[pl.BlockSpec((tm, k), lambda i, j: (i, 0)),
                  pl.BlockSpec((None, k, tn), lambda i, j: (l, 0, j))],
        out_specs=pl.BlockSpec((tm, tn), lambda i, j: (i, j)),
        out_shape=jax.ShapeDtypeStruct((m, n_out), out_dtype),
        compiler_params=_cparams("parallel", "parallel"),
        name="matmul",
    )(a, b)


def _mm_res_body(a_ref, b_ref, res_ref, mod_ref, o_ref, *, gate_idx):
    acc = jnp.dot(a_ref[...], b_ref[...], preferred_element_type=f32)
    o_ref[...] = res_ref[...] + mod_ref[gate_idx:gate_idx + 1, :] * acc


def matmul_gated_residual(a, b, l, res, mod, lay, gate_idx, tm, tn):
    m, k = a.shape
    n = res.shape[1]
    seq = lambda i: lay.seq_of_row(i * tm)
    return pl.pallas_call(
        functools.partial(_mm_res_body, gate_idx=gate_idx),
        grid=(m // tm, n // tn),
        in_specs=[pl.BlockSpec((tm, k), lambda i, j: (i, 0)),
                  pl.BlockSpec((None, k, tn), lambda i, j: (l, 0, j)),
                  pl.BlockSpec((tm, tn), lambda i, j: (i, j)),
                  pl.BlockSpec((None, None, N_MOD, tn), lambda i, j: (l, seq(i), 0, j))],
        out_specs=pl.BlockSpec((tm, tn), lambda i, j: (i, j)),
        out_shape=jax.ShapeDtypeStruct((m, n), f32),
        compiler_params=_cparams("parallel", "parallel"),
        name="matmul_gated_residual",
    )(a, b, res, mod)


POS_BLOCK = 1024
POS_LANE0 = ATT_QK
LOG2E = math.log2(math.e)


def _attn_pre_body(q_ref, k_ref, v_ref, g_ref, qn_ref, kn_ref, vb_ref, *, tm, lay):
    lane = lax.broadcasted_iota(jnp.int32, q_ref.shape, 1)
    lo = lane < ATT_QK

    def half_norm(x, g):
        sq = x * x
        s_lo = jnp.sum(jnp.where(lo, sq, 0.0), axis=-1, keepdims=True)
        s_hi = jnp.sum(jnp.where(lo, 0.0, sq), axis=-1, keepdims=True)
        inv = jnp.where(lo, lax.rsqrt(s_lo / ATT_QK + EPS), lax.rsqrt(s_hi / ATT_QK + EPS))
        return x * inv * g

    qn = half_norm(q_ref[...], g_ref[0:1, :]) * (ATT_QK ** -0.5 * LOG2E)
    kn = half_norm(k_ref[...], g_ref[1:2, :])
    row = pl.program_id(0) * tm + lax.broadcasted_iota(jnp.int32, q_ref.shape, 0)
    r = jnp.where(row < lay.p, row % lay.s1, (row - lay.p) % lay.s2) % POS_BLOCK
    hi = ((r // LANES) * LANES).astype(f32)
    lo_part = (r % LANES).astype(f32)
    pos = jnp.where((lane >= POS_LANE0) & (lane < POS_LANE0 + 3), hi,
                    jnp.where((lane >= POS_LANE0 + 3) & (lane < POS_LANE0 + 6), lo_part, 0.0))
    for mi in range(2):
        qm = qn if mi == 0 else pltpu.roll(qn, ATT_QK, axis=1)
        km = kn if mi == 0 else pltpu.roll(kn, ATT_QK, axis=1)
        qn_ref[mi] = jnp.where(lo, qm, 0.0).astype(bf16)
        kn_ref[mi] = jnp.where(lo, km, pos).astype(bf16)
    vb_ref[...] = v_ref[...].astype(bf16)


def attn_prepare(proj, qk_g2, lay, tm):
    t = proj.shape[0]
    hd = ATT_HEADS
    blk = lambda off: pl.BlockSpec((tm, ATT_V), lambda i, h: (i, off + h))
    qk_spec = pl.BlockSpec((None, 2, tm, ATT_V), lambda i, h: (h, 0, i, 0))
    qk_shp = jax.ShapeDtypeStruct((hd, 2, t, ATT_V), bf16)
    return pl.pallas_call(
        functools.partial(_attn_pre_body, tm=tm, lay=lay),
        grid=(t // tm, hd),
        in_specs=[blk(0), blk(hd), blk(2 * hd), pl.BlockSpec((2, ATT_V), lambda i, h: (0, 0))],
        out_specs=[qk_spec, qk_spec, pl.BlockSpec((None, tm, ATT_V), lambda i, h: (h, i, 0))],
        out_shape=[qk_shp, qk_shp, jax.ShapeDtypeStruct((hd, t, ATT_V), bf16)],
        compiler_params=_cparams("parallel", "parallel"),
        name="attn_prepare",
    )(proj, proj, proj, qk_g2)


def alibi_slope_lanes():
    slopes = 2.0 ** (-8.0 * np.arange(1, ATT_HEADS + 1) / ATT_HEADS) * LOG2E
    c = jnp.asarray(slopes, f32)
    c1 = c.astype(bf16)
    c2 = (c - c1.astype(f32)).astype(bf16)
    c3 = (c - c1.astype(f32) - c2.astype(f32)).astype(bf16)
    pieces = jnp.stack([c1, c2, c3, c1, c2, c3], axis=1)
    lanes = jnp.zeros((ATT_HEADS, 1, ATT_V), bf16).at[:, 0, POS_LANE0:POS_LANE0 + 6].set(pieces)
    return c, lanes


def _attn_body(scal_ref, q_ref, k_ref, v_ref, cl_ref, g_ref, o_ref, m_ref, l_ref, acc_ref, s_ref,
               *, tq, tk, tw, seq_len, out_scale):
    h, qi = pl.program_id(0), pl.program_id(1)
    c = scal_ref[h]
    qpos0 = (qi * tq) % seq_len
    ipos = (qpos0 + lax.broadcasted_iota(jnp.int32, (tq, 1), 0)).astype(f32)
    lane = lax.broadcasted_iota(jnp.int32, (tq, ATT_V), 1)
    pos_lane = (lane >= POS_LANE0) & (lane < POS_LANE0 + 6)
    nt = (((1,), (1,)), ((), ()))
    m_ref[...] = jnp.full(m_ref.shape, -jnp.inf, f32)
    l_ref[...] = jnp.zeros(l_ref.shape, f32)
    acc_ref[...] = jnp.zeros(acc_ref.shape, f32)
    cl = cl_ref[...]

    def wide_tile(wb, carry):
        w0 = pl.multiple_of(wb * tw, tw)
        for sub in range(tw // tk):
            j0 = w0 + sub * tk
            cols = slice(sub * tk, (sub + 1) * tk)
            is_left = j0 + tk <= qpos0
            is_right = j0 >= qpos0 + tq
            sgn = jnp.where(is_left, 1.0, jnp.where(is_right, -1.0, 0.0))
            off = (sgn * c) * (((j0 // POS_BLOCK) * POS_BLOCK).astype(f32) - ipos)
            for mi in range(2):
                qv = jnp.where(pos_lane, sgn.astype(cl.dtype) * cl, q_ref[mi])
                s_ref[mi, :, cols] = lax.dot_general(qv, k_ref[mi, pl.ds(j0, tk), :], nt,
                                                     preferred_element_type=f32) + off

            @pl.when(jnp.logical_not(is_left | is_right))
            def _():
                rc = lax.broadcasted_iota(jnp.int32, (tq, tk), 0) - lax.broadcasted_iota(jnp.int32, (tq, tk), 1)
                bias = c * jnp.abs(rc + (qpos0 - j0)).astype(f32)
                for mi in range(2):
                    s_ref[mi, :, cols] = s_ref[mi, :, cols] - bias

        vt = v_ref[pl.ds(w0, tw), :]
        for mi in range(2):
            s = s_ref[mi]
            m_prev = m_ref[mi]
            m_new = jnp.maximum(m_prev, jnp.max(s, axis=-1, keepdims=True))
            alpha = jnp.exp2(m_prev - m_new)
            p = jnp.exp2(s - m_new)
            l_ref[mi] = alpha * l_ref[mi] + jnp.sum(p, axis=-1, keepdims=True)
            acc_ref[mi] = alpha * acc_ref[mi] + jnp.dot(p.astype(bf16), vt, preferred_element_type=f32)
            m_ref[mi] = m_new
        return carry

    lax.fori_loop(0, seq_len // tw, wide_tile, 0)

    lam = scal_ref[ATT_HEADS]
    o = acc_ref[0] / l_ref[0] - lam * (acc_ref[1] / l_ref[1])
    ms = jnp.mean(o * o, axis=-1, keepdims=True)
    o_ref[...] = (o * lax.rsqrt(ms + EPS) * g_ref[...] * out_scale).astype(o_ref.dtype)


def attention_group(scal, c_lanes, qn, kn, vb, out_g, row_start, seq_len, nseq, tq, tk, tw, out_scale):
    tq, tk, tw = min(tq, seq_len), min(tk, seq_len), min(tw, seq_len)
    assert POS_BLOCK % tk == 0 and tw % tk == 0
    q0, s0 = row_start // tq, row_start // seq_len
    rows = nseq * seq_len
    return pl.pallas_call(
        functools.partial(_attn_body, tq=tq, tk=tk, tw=tw, seq_len=seq_len, out_scale=out_scale),
        grid=(ATT_HEADS, rows // tq),
        in_specs=[pl.BlockSpec(memory_space=pltpu.SMEM),
                  pl.BlockSpec((None, 2, tq, ATT_V), lambda h, qi: (h, 0, q0 + qi, 0)),
                  pl.BlockSpec((None, 2, seq_len, ATT_V), lambda h, qi: (h, 0, s0 + (qi * tq) // seq_len, 0)),
                  pl.BlockSpec((None, seq_len, ATT_V), lambda h, qi: (h, s0 + (qi * tq) // seq_len, 0)),
                  pl.BlockSpec((None, 1, ATT_V), lambda h, qi: (h, 0, 0)),
                  pl.BlockSpec((1, ATT_V), lambda h, qi: (0, 0))],
        out_specs=pl.BlockSpec((tq, ATT_V), lambda h, qi: (qi, h)),
        out_shape=jax.ShapeDtypeStruct((rows, ATT_W), bf16),
        scratch_shapes=[pltpu.VMEM((2, tq, 1), f32), pltpu.VMEM((2, tq, 1), f32),
                        pltpu.VMEM((2, tq, ATT_V), f32), pltpu.VMEM((2, tq, tw), f32)],
        compiler_params=_cparams("parallel", "parallel"),
        name="diff_attention",
    )(scal, qn, kn, vb, c_lanes, out_g)


def _mlstm_body(q_ref, k_ref, v_ref, gc_ref, gr_ref, h_ref, c_ref, n_ref, m_ref, *, lc, lay, backward):
    c = pl.program_id(0)
    nchunks = pl.num_programs(0)
    heads = ML_HEADS
    ceff = nchunks - 1 - c if backward else c
    r0 = ceff * lc
    reset = lay.is_boundary(r0 + lc) if backward else lay.is_boundary(r0)

    @pl.when(reset)
    def _():
        c_ref[...] = jnp.zeros(c_ref.shape, f32)
        n_ref[...] = jnp.zeros(n_ref.shape, f32)
        m_ref[...] = jnp.zeros(m_ref.shape, f32)

    row = lax.broadcasted_iota(jnp.int32, (lc, lc), 0)
    col = lax.broadcasted_iota(jnp.int32, (lc, lc), 1)
    mask = (col >= row) if backward else (col <= row)
    gcol = gc_ref[...]
    lane = lax.broadcasted_iota(jnp.int32, gcol.shape, 1)
    d = 1 if backward else 0

    def col_of(idx):
        return jnp.sum(jnp.where(lane == idx, gcol, 0.0), axis=-1, keepdims=True)

    for h in range(heads):
        ia = d * heads + h
        a_col, i_col = col_of(ia), col_of(2 * heads + ia)
        a_row = gr_ref[ia:ia + 1, :]
        i_row = gr_ref[2 * heads + ia:2 * heads + ia + 1, :]
        g11 = gr_ref[4 * heads + ia:4 * heads + ia + 1, 0:1]
        m11 = m_ref[h][:, 0:1]
        qf = q_ref[:, h * ML_QK:(h + 1) * ML_QK] * (ML_QK ** -0.5)
        kf = k_ref[:, h * ML_QK:(h + 1) * ML_QK]
        qb, kb = qf.astype(bf16), kf.astype(bf16)
        vb = v_ref[:, h * ML_V:(h + 1) * ML_V].astype(bf16)
        dmat = jnp.where(mask, a_col - a_row + i_row, -jnp.inf)
        m_inter = a_col + m11
        mj = jnp.maximum(jnp.max(dmat, axis=-1, keepdims=True), m_inter)
        p = jnp.exp(dmat - mj)
        s = lax.dot_general(qb, kb, (((1,), (1,)), ((), ())), preferred_element_type=f32)
        wts = p * s
        inter = jnp.exp(m_inter - mj)
        cmat = c_ref[h]
        num = (jnp.dot(wts.astype(bf16), vb, preferred_element_type=f32)
               + inter * jnp.dot(qb, cmat.astype(bf16), preferred_element_type=f32))
        nvec = n_ref[h]
        den = jnp.sum(wts, axis=-1, keepdims=True) + inter * jnp.sum(qf * nvec, axis=-1, keepdims=True)
        h_ref[:, h * ML_V:(h + 1) * ML_V] = num / jnp.maximum(jnp.abs(den), jnp.exp(-mj))
        ws = g11 - a_col + i_col
        m_new = jnp.maximum(g11 + m11, jnp.max(ws, axis=0, keepdims=True))
        dec = jnp.exp(g11 + m11 - m_new)
        ek = jnp.exp(ws - m_new) * kf
        c_ref[h] = dec * cmat + lax.dot_general(ek.astype(bf16), vb, (((0,), (0,)), ((), ())),
                                                preferred_element_type=f32)
        n_ref[h] = dec * nvec + jnp.sum(ek, axis=0, keepdims=True)
        m_ref[h] = jnp.broadcast_to(m_new, (1, LANES))


def mlstm_scan(proj, gcol, grow, lay, lc, backward, q_off, k_off, v_off):
    t = proj.shape[0]
    nchunks = t // lc
    hq, hv = ML_HEADS * ML_QK, ML_W
    cidx = (lambda c: nchunks - 1 - c) if backward else (lambda c: c)
    return pl.pallas_call(
        functools.partial(_mlstm_body, lc=lc, lay=lay, backward=backward),
        grid=(nchunks,),
        in_specs=[pl.BlockSpec((lc, hq), lambda c: (cidx(c), q_off // hq)),
                  pl.BlockSpec((lc, hq), lambda c: (cidx(c), k_off // hq)),
                  pl.BlockSpec((lc, hv), lambda c: (cidx(c), v_off // hv)),
                  pl.BlockSpec((lc, LANES), lambda c: (cidx(c), 0)),
                  pl.BlockSpec((grow.shape[0], lc), lambda c: (0, cidx(c)))],
        out_specs=pl.BlockSpec((lc, hv), lambda c: (cidx(c), 0)),
        out_shape=jax.ShapeDtypeStruct((t, hv), f32),
        scratch_shapes=[pltpu.VMEM((ML_HEADS, ML_QK, ML_V), f32), pltpu.VMEM((ML_HEADS, 1, ML_QK), f32),
                        pltpu.VMEM((ML_HEADS, 1, LANES), f32)],
        compiler_params=_cparams("arbitrary"),
        name="mlstm_bwd" if backward else "mlstm_fwd",
    )(proj, proj, proj, gcol, grow)


def _mlstm_post_body(hf_ref, hb_ref, o_ref, g_ref, y_ref):
    hsum = hf_ref[...] + hb_ref[...]
    ms = jnp.mean(hsum * hsum, axis=-1, keepdims=True)
    y_ref[...] = (jax.nn.sigmoid(o_ref[...]) * (hsum * lax.rsqrt(ms + EPS) * g_ref[...])).astype(y_ref.dtype)


def mlstm_post(hf, hb, proj, o_off, ml_g, l, tm):
    t = hf.shape[0]
    blk = lambda off: pl.BlockSpec((tm, ML_V), lambda i, h: (i, off + h))
    return pl.pallas_call(
        _mlstm_post_body,
        grid=(t // tm, ML_HEADS),
        in_specs=[blk(0), blk(0), blk(o_off // ML_V), pl.BlockSpec((None, 1, ML_V), lambda i, h: (l, 0, h))],
        out_specs=blk(0),
        out_shape=jax.ShapeDtypeStruct((t, ML_W), bf16),
        compiler_params=_cparams("parallel", "parallel"),
        name="mlstm_post",
    )(hf, hb, proj, ml_g.reshape(DEPTH, 1, ML_W))


def mlstm_gate_tables(gates, gate_b, lc):
    t = gates.shape[0]
    hd = ML_HEADS
    g = gates[:, :4 * hd].reshape(t, 4, hd) + gate_b.astype(f32)
    i_f, f_f, i_b, f_b = g[:, 0], g[:, 1], g[:, 2], g[:, 3]
    lf_f = jax.nn.log_sigmoid(f_f).reshape(t // lc, lc, hd)
    lf_b = jax.nn.log_sigmoid(f_b).reshape(t // lc, lc, hd)
    a_f = jnp.cumsum(lf_f, axis=1)
    a_b = jnp.flip(jnp.cumsum(jnp.flip(lf_b, axis=1), axis=1), axis=1)
    tot_f = jnp.broadcast_to(a_f[:, -1:, :], a_f.shape)
    tot_b = jnp.broadcast_to(a_b[:, :1, :], a_b.shape)
    flat = lambda a: a.reshape(t, hd)
    cols = jnp.concatenate([flat(a_f), flat(a_b), i_f, i_b, flat(tot_f), flat(tot_b)], axis=1)
    gcol = jnp.pad(cols, ((0, 0), (0, LANES - cols.shape[1])))
    grow = jnp.pad(cols.T, ((0, (-cols.shape[1]) % 8), (0, 0)))
    return gcol, grow


def _hyconv_body(c_ref, p_ref, n_ref, w_ref, b_ref, u_ref, ub_ref, *, tm, lay):
    r0 = pl.program_id(0) * tm
    x = c_ref[...]
    prev_row = jnp.where(lay.is_boundary(r0), 0.0, p_ref[7:8, :])
    next_row = jnp.where(lay.is_boundary(r0 + tm), 0.0, n_ref[0:1, :])
    rid = lax.broadcasted_iota(jnp.int32, x.shape, 0)
    xm = jnp.where(rid == 0, prev_row, pltpu.roll(x, 1, axis=0))
    xp = jnp.where(rid == tm - 1, next_row, pltpu.roll(x, tm - 1, axis=0))
    u = xm * w_ref[0:1, :] + x * w_ref[1:2, :] + xp * w_ref[2:3, :] + b_ref[...]
    u_ref[...] = u
    ub_ref[...] = u.astype(bf16)


def hyena_short_conv(proj, hy_off, conv_w, conv_b, l, lay, tm, tc):
    t = proj.shape[0]
    width = 3 * HY_W
    c0 = hy_off // tc
    nblk8 = t // 8
    return pl.pallas_call(
        functools.partial(_hyconv_body, tm=tm, lay=lay),
        grid=(t // tm, width // tc),
        in_specs=[pl.BlockSpec((tm, tc), lambda i, j: (i, c0 + j)),
                  pl.BlockSpec((8, tc), lambda i, j: (jnp.maximum(i * (tm // 8) - 1, 0), c0 + j)),
                  pl.BlockSpec((8, tc), lambda i, j: (jnp.minimum((i + 1) * (tm // 8), nblk8 - 1), c0 + j)),
                  pl.BlockSpec((None, 3, tc), lambda i, j: (l, 0, j)),
                  pl.BlockSpec((None, 1, tc), lambda i, j: (l, 0, j))],
        out_specs=[pl.BlockSpec((tm, tc), lambda i, j: (i, j))] * 2,
        out_shape=[jax.ShapeDtypeStruct((t, width), f32), jax.ShapeDtypeStruct((t, width), bf16)],
        compiler_params=_cparams("parallel", "parallel"),
        name="hyena_short_conv",
    )(proj, proj, proj, conv_w, conv_b.reshape(DEPTH, 1, width))


def hyena_time_filters(seq_len, w1, b1, freq, w2, b2, w3, b3):
    t = jnp.linspace(0.0, 1.0, seq_len, dtype=f32)[:, None]
    bands = (HY_EMB - 1) // 2
    fr = jnp.linspace(1e-4, bands - 1, bands, dtype=f32)
    ang = (2.0 * math.pi / seq_len) * jnp.arange(seq_len, dtype=f32)[:, None] * fr[None, :]
    z = jnp.concatenate([t, jnp.cos(ang), -jnp.sin(ang)], axis=-1)
    fq = freq.astype(f32)
    h = jnp.sin(fq[0] * (jnp.dot(z, w1, precision=HI) + b1))
    h = jnp.sin(fq[1] * (jnp.dot(h, w2, precision=HI) + b2))
    h = (jnp.dot(h, w3, precision=HI) + b3).reshape(seq_len, HY_ORDER, 2, HY_W)
    deltas = jnp.linspace(math.log(HY_TARGET) / HY_SLOW, math.log(HY_TARGET) / HY_FAST, HY_W, dtype=f32)
    h = h * jnp.exp(-t * jnp.abs(deltas))[:, None, None, :]
    hf, hb = h[:, :, 0], h[:, :, 1]
    first = (jnp.arange(seq_len) == 0)[:, None, None]
    hf = jnp.where(first, hf + hb, hf)
    hb = jnp.where(first, 0.0, hb)
    p = (hf + hb).reshape(seq_len, HY_ORDER * HY_W)
    q = (hf - hb).reshape(seq_len, HY_ORDER * HY_W)
    return p.astype(bf16), q.astype(bf16)


def dft_matrices(seq_len):
    k = jnp.arange(seq_len, dtype=jnp.int32)[:, None]

    def table(n):
        ang = ((k * n[None, :]) % (2 * seq_len)).astype(f32) * (math.pi / seq_len)
        return jnp.cos(ang), jnp.sin(ang)

    ca, sa = table(jnp.arange(seq_len // LANES, dtype=jnp.int32) * LANES)
    cb, sb = table(jnp.arange(LANES, dtype=jnp.int32))
    ca, sa, cb, sb = ca[:, :, None], sa[:, :, None], cb[:, None, :], sb[:, None, :]
    cos_m = (ca * cb - sa * sb).reshape(seq_len, seq_len).astype(bf16)
    nsin_m = (-(sa * cb + ca * sb)).reshape(seq_len, seq_len).astype(bf16)
    alt = jnp.where(jnp.arange(seq_len) % 2 == 0, 1.0, -1.0).astype(bf16)
    alt8 = jnp.zeros((8, seq_len), bf16).at[0].set(alt)
    return cos_m, nsin_m, alt8


def _hy_filter_body(fc_ref, fs_ref, alt_ref, p_ref, q_ref, hr_ref, hi_ref):
    p = p_ref[...]
    hr_ref[...] = jnp.dot(fc_ref[...], p, preferred_element_type=f32)
    hi = jnp.dot(fs_ref[...], q_ref[...], preferred_element_type=f32)
    nyq = jnp.dot(alt_ref[...], p, preferred_element_type=f32)[0:1, :]
    grow = pl.program_id(0) * hr_ref.shape[0] + lax.broadcasted_iota(jnp.int32, hi.shape, 0)
    hi_ref[...] = jnp.where(grow == 0, nyq, hi)


def hyena_filter_spectrum(cos_m, nsin_m, alt8, p, q, tm, tn):
    seq_len, n = p.shape
    a_spec = pl.BlockSpec((tm, seq_len), lambda i, j: (i, 0))
    b_spec = pl.BlockSpec((seq_len, tn), lambda i, j: (0, j))
    o_spec = pl.BlockSpec((tm, tn), lambda i, j: (i, j))
    shp = jax.ShapeDtypeStruct((seq_len, n), f32)
    return pl.pallas_call(
        _hy_filter_body,
        grid=(seq_len // tm, n // tn),
        in_specs=[a_spec, a_spec, pl.BlockSpec((8, seq_len), lambda i, j: (0, 0)), b_spec, b_spec],
        out_specs=[o_spec, o_spec], out_shape=[shp, shp],
        compiler_params=_cparams("parallel", "parallel"),
        name="hyena_filter_spectrum",
    )(cos_m, nsin_m, alt8, p, q)


def _hy_fwd_body(fc_ref, fs_ref, alt_ref, z_ref, hr_ref, hi_ref, yr_ref, yi_ref, *, n_fft):
    z = z_ref[...]
    vr = jnp.dot(fc_ref[...], z, preferred_element_type=f32)
    vi = jnp.dot(fs_ref[...], z, preferred_element_type=f32)
    nyq = jnp.dot(alt_ref[...], z, preferred_element_type=f32)[0:1, :]
    hr, hi = hr_ref[...], hi_ref[...]
    grow = pl.program_id(0) * vr.shape[0] + lax.broadcasted_iota(jnp.int32, vr.shape, 0)
    row0 = grow == 0
    vihi = jnp.where(row0, nyq, vi) * hi
    yr = vr * hr - jnp.where(row0, 0.0, vihi)
    yi = jnp.where(row0, vihi, vr * hi + vi * hr)
    scale = jnp.where(row0, 1.0 / n_fft, 2.0 / n_fft)
    yr_ref[...] = (yr * scale).astype(bf16)
    yi_ref[...] = (yi * scale).astype(bf16)


def hyena_fwd(cos_m, nsin_m, alt8, zb, z_col, hr, hi, order, row_start, seq_len, nseq, tm, tn):
    tm, tn = min(tm, seq_len), min(tn, HY_W)
    rows = nseq * seq_len
    s0, i0 = row_start // seq_len, seq_len // tm
    a_spec = pl.BlockSpec((tm, seq_len), lambda i, s, j: (i, 0))
    h_spec = pl.BlockSpec((tm, tn), lambda i, s, j: (i, order * (HY_W // tn) + j))
    o_spec = pl.BlockSpec((tm, tn), lambda i, s, j: (s * i0 + i, j))
    shp = jax.ShapeDtypeStruct((rows, HY_W), bf16)
    return pl.pallas_call(
        functools.partial(_hy_fwd_body, n_fft=2 * seq_len),
        grid=(seq_len // tm, nseq, HY_W // tn),
        in_specs=[a_spec, a_spec, pl.BlockSpec((8, seq_len), lambda i, s, j: (0, 0)),
                  pl.BlockSpec((seq_len, tn), lambda i, s, j: (s0 + s, z_col // tn + j)),
                  h_spec, h_spec],
        out_specs=[o_spec, o_spec], out_shape=[shp, shp],
        compiler_params=_cparams("parallel", "parallel", "parallel"),
        name="hyena_fwd_dft",
    )(cos_m, nsin_m, alt8, zb, hr, hi)


def _hy_inv_body(fc_ref, ft_ref, yr_ref, yi_ref, z_ref, x_ref, d_ref, *rest, last):
    y = (jnp.dot(fc_ref[...], yr_ref[...], preferred_element_type=f32)
         + jnp.dot(ft_ref[...], yi_ref[...], preferred_element_type=f32))
    n = pl.program_id(0) * y.shape[0] + lax.broadcasted_iota(jnp.int32, (y.shape[0], 1), 0)
    y = y + jnp.where(n % 2 == 0, 1.0, -1.0) * yi_ref[0:1, :].astype(f32)
    z = x_ref[...] * (y + d_ref[...] * z_ref[...])
    if last:
        g_ref, o_ref = rest
        gw = HY_W // HY_GROUPS
        for gi in range(z.shape[1] // gw):
            seg = z[:, gi * gw:(gi + 1) * gw]
            ms = jnp.mean(seg * seg, axis=-1, keepdims=True)
            o_ref[:, gi * gw:(gi + 1) * gw] = (seg * lax.rsqrt(ms + EPS)
                                               * g_ref[:, gi * gw:(gi + 1) * gw]).astype(o_ref.dtype)
    else:
        o_ref, ob_ref = rest
        o_ref[...] = z
        ob_ref[...] = z.astype(bf16)


def hyena_inv(cos_m, nsin_m, yr, yi, z_arr, z_row0, z_col, u, x_col, d_bias, gain, l, order,
              row_start, seq_len, nseq, tm, tn):
    tm, tn = min(tm, seq_len), min(tn, HY_W)
    last = order == HY_ORDER - 1
    rows = nseq * seq_len
    i0 = seq_len // tm
    a_spec = pl.BlockSpec((tm, seq_len), lambda i, s, j: (i, 0))
    b_spec = pl.BlockSpec((seq_len, tn), lambda i, s, j: (s, j))
    e_spec = lambda row0, col: pl.BlockSpec((tm, tn), lambda i, s, j: (row0 // tm + s * i0 + i, col // tn + j))
    o_spec = pl.BlockSpec((tm, tn), lambda i, s, j: (s * i0 + i, j))
    in_specs = [a_spec, a_spec, b_spec, b_spec, e_spec(z_row0, z_col), e_spec(row_start, x_col),
                pl.BlockSpec((None, 1, tn), lambda i, s, j: (l * HY_ORDER + order, 0, j))]
    args = [cos_m, nsin_m, yr, yi, z_arr, u, d_bias.reshape(DEPTH * HY_ORDER, 1, HY_W)]
    if last:
        in_specs.append(pl.BlockSpec((None, 1, tn), lambda i, s, j: (l, 0, j)))
        args.append(gain.reshape(DEPTH, 1, HY_W))
        out_specs = o_spec
        out_shape = jax.ShapeDtypeStruct((rows, HY_W), bf16)
    else:
        out_specs = [o_spec, o_spec]
        out_shape = [jax.ShapeDtypeStruct((rows, HY_W), f32), jax.ShapeDtypeStruct((rows, HY_W), bf16)]
    return pl.pallas_call(
        functools.partial(_hy_inv_body, last=last),
        grid=(seq_len // tm, nseq, HY_W // tn),
        in_specs=in_specs, out_specs=out_specs, out_shape=out_shape,
        compiler_params=_cparams("parallel", "parallel", "parallel"),
        name="hyena_inv_dft",
    )(*args)


def hyena_group(u, ub, spec, d_bias, gain, l, row_start, seq_len, nseq):
    cos_m, nsin_m, alt8, hr, hi = spec
    tm, tn = (TM_DFT, TN_DFT) if seq_len > DFT_SHORT else (2 * TM_DFT, 2 * TN_DFT)
    grp = (row_start, seq_len, nseq, tm, tn)
    yr, yi = hyena_fwd(cos_m, nsin_m, alt8, ub, 0, hr, hi, 0, *grp)
    z1, z1b = hyena_inv(cos_m, nsin_m, yr, yi, u, row_start, 0, u, HY_W, d_bias, gain, l, 0, *grp)
    yr, yi = hyena_fwd(cos_m, nsin_m, alt8, z1b, 0, hr, hi, 1, 0, seq_len, nseq, tm, tn)
    return hyena_inv(cos_m, nsin_m, yr, yi, z1, 0, 0, u, 2 * HY_W, d_bias, gain, l, 1, *grp)


def _item_changed(w, e_ref, c_ref):
    prev = jnp.maximum(w - 1, 0)
    return (w == 0) | (e_ref[w] != e_ref[prev]) | (c_ref[w] != c_ref[prev])


def _ffn_up_body(ib_ref, ic_ref, ie_ref, n_ref, x_ref, wg_ref, wu_ref, h_ref, wgb_ref, wub_ref):
    w = pl.program_id(0)

    @pl.when(w < n_ref[0])
    def _():
        @pl.when(_item_changed(w, ie_ref, ic_ref))
        def _():
            wgb_ref[...] = wg_ref[...].astype(bf16)
            wub_ref[...] = wu_ref[...].astype(bf16)

        x = x_ref[...]
        g = jnp.dot(x, wgb_ref[...], preferred_element_type=f32)
        u = jnp.dot(x, wub_ref[...], preferred_element_type=f32)
        h_ref[...] = (g * jax.nn.sigmoid(g) * u).astype(h_ref.dtype)


def _ffn_down_body(ib_ref, ic_ref, ie_ref, n_ref, h_ref, wd_ref, y_ref, wdb_ref):
    w = pl.program_id(0)

    @pl.when(w < n_ref[0])
    def _():
        @pl.when(_item_changed(w, ie_ref, ic_ref))
        def _():
            wdb_ref[...] = wd_ref[...].astype(bf16)

        y_ref[...] = jnp.dot(h_ref[...], wdb_ref[...], preferred_element_type=f32).astype(y_ref.dtype)


def ffn_items(plan, nb, ncol):
    blocks_per_e, first_block_e, n_used = plan
    n_items = n_used[0] * ncol
    w = jnp.minimum(jnp.arange(nb * ncol, dtype=jnp.int32), n_items - 1)
    istart = first_block_e * ncol
    e = jnp.sum((istart[None, :] <= w[:, None]).astype(jnp.int32), axis=1) - 1
    local = w - istart[e]
    cnt = jnp.maximum(blocks_per_e[e], 1)
    item_c = (local // cnt).astype(jnp.int32)
    item_b = (first_block_e[e] + local % cnt).astype(jnp.int32)
    return item_b, item_c, e, n_items[None].astype(jnp.int32)


def grouped_ffn(x_rows, plan, wg, wu, wd, l, tb, tf, tn, out_dtype):
    rows, d = x_rows.shape
    fdim = wg.shape[3]
    nb = rows // tb
    h = pl.pallas_call(
        _ffn_up_body,
        grid_spec=pltpu.PrefetchScalarGridSpec(
            num_scalar_prefetch=4, grid=(nb * (fdim // tf),),
            in_specs=[pl.BlockSpec((tb, d), lambda w, ib, ic, ie, n: (ib[w], 0)),
                      pl.BlockSpec((None, None, d, tf), lambda w, ib, ic, ie, n: (l, ie[w], 0, ic[w])),
                      pl.BlockSpec((None, None, d, tf), lambda w, ib, ic, ie, n: (l, ie[w], 0, ic[w]))],
            out_specs=pl.BlockSpec((tb, tf), lambda w, ib, ic, ie, n: (ib[w], ic[w])),
            scratch_shapes=[pltpu.VMEM((d, tf), bf16), pltpu.VMEM((d, tf), bf16)]),
        out_shape=jax.ShapeDtypeStruct((rows, fdim), bf16),
        compiler_params=_cparams("arbitrary"),
        name="ffn_up",
    )(*ffn_items(plan, nb, fdim // tf), x_rows, wg, wu)
    return pl.pallas_call(
        _ffn_down_body,
        grid_spec=pltpu.PrefetchScalarGridSpec(
            num_scalar_prefetch=4, grid=(nb * (d // tn),),
            in_specs=[pl.BlockSpec((tb, fdim), lambda w, ib, ic, ie, n: (ib[w], 0)),
                      pl.BlockSpec((None, None, fdim, tn), lambda w, ib, ic, ie, n: (l, ie[w], 0, ic[w]))],
            out_specs=pl.BlockSpec((tb, tn), lambda w, ib, ic, ie, n: (ib[w], ic[w])),
            scratch_shapes=[pltpu.VMEM((fdim, tn), bf16)]),
        out_shape=jax.ShapeDtypeStruct((rows, d), out_dtype),
        compiler_params=_cparams("arbitrary"),
        name="ffn_down",
    )(*ffn_items(plan, nb, d // tn), h, wd)


def route_tokens(logits, router_b):
    t = logits.shape[0]
    per = N_EXPERTS // N_GROUPS
    s = jax.nn.sigmoid(logits)
    sb = s + router_b.astype(f32)

    def pop_max(x):
        m = jnp.max(x, axis=-1, keepdims=True)
        iota = lax.broadcasted_iota(jnp.int32, x.shape, x.ndim - 1)
        first = jnp.min(jnp.where(x == m, iota, x.shape[-1]), axis=-1, keepdims=True)
        hit = iota == first
        return m, first, hit, jnp.where(hit, -jnp.inf, x)

    m1, _, _, rest = pop_max(sb.reshape(t, N_GROUPS, per))
    gscore = (m1 + jnp.max(rest, axis=-1, keepdims=True))[..., 0]
    gmask = jnp.zeros(gscore.shape, bool)
    for _ in range(TOPK_GROUPS):
        _, _, hit, gscore = pop_max(gscore)
        gmask = gmask | hit
    cur = jnp.where(jnp.repeat(gmask, per, axis=1), sb, -jnp.inf)
    idx, gate = [], []
    for _ in range(TOP_K):
        _, first, hit, cur = pop_max(cur)
        idx.append(first)
        gate.append(jnp.sum(jnp.where(hit, s, 0.0), axis=-1, keepdims=True))
    idx = jnp.concatenate(idx, axis=1)
    gate = jnp.concatenate(gate, axis=1)
    gate = gate / jnp.sum(gate, axis=-1, keepdims=True) * ROUTED_SCALE
    return idx, gate


def _row_gather_body(nused_ref, tok_ref, src_ref, out_ref, buf_ref, sem):
    tb = out_ref.shape[0]

    @pl.when(pl.program_id(0) < nused_ref[0])
    def _():
        def issue(r, carry):
            pltpu.make_async_copy(src_ref.at[pl.ds(tok_ref[0, r], 1)], buf_ref.at[pl.ds(r, 1)], sem).start()
            return carry

        lax.fori_loop(0, tb, issue, 0, unroll=8)
        pltpu.make_async_copy(src_ref.at[pl.ds(0, tb)], buf_ref, sem).wait()
        out_ref[...] = buf_ref[...].astype(out_ref.dtype)


def gather_rows(src, row_tok, n_used, tb):
    rows = row_tok.shape[0]
    d = src.shape[1]
    nb = rows // tb
    return pl.pallas_call(
        _row_gather_body,
        grid_spec=pltpu.PrefetchScalarGridSpec(
            num_scalar_prefetch=1, grid=(nb,),
            in_specs=[pl.BlockSpec((None, 1, tb), lambda b, n: (b, 0, 0), memory_space=pltpu.SMEM),
                      pl.BlockSpec(memory_space=pl.ANY)],
            out_specs=pl.BlockSpec((tb, d), lambda b, n: (b, 0)),
            scratch_shapes=[pltpu.VMEM((tb, d), src.dtype), pltpu.SemaphoreType.DMA(())]),
        out_shape=jax.ShapeDtypeStruct((rows, d), bf16),
        compiler_params=_cparams("arbitrary"),
        name="gather_rows",
    )(n_used, row_tok.reshape(nb, 1, tb), src)


def dispatch_plan(idx, tb):
    t = idx.shape[0]
    n = t * TOP_K
    e_flat = idx.reshape(n).astype(jnp.int32)
    onehot = (e_flat[:, None] == jnp.arange(N_EXPERTS, dtype=jnp.int32)[None, :]).astype(jnp.int32)
    csum = jnp.cumsum(onehot, axis=0)
    counts = csum[-1]
    rank = jnp.take_along_axis(csum, e_flat[:, None], axis=1)[:, 0] - 1
    padded = (counts + tb - 1) // tb * tb
    pends = jnp.cumsum(padded)
    pstarts = pends - padded
    pos = pstarts[e_flat] + rank
    nb = (n + N_EXPERTS * (tb - 1) + tb - 1) // tb
    tok = jnp.arange(n, dtype=jnp.int32) // TOP_K
    row_tok = jnp.zeros((nb * tb,), jnp.int32).at[pos].set(tok, unique_indices=True)
    plan = ((padded // tb).astype(jnp.int32), (pstarts // tb).astype(jnp.int32),
            (pends[-1:] // tb).astype(jnp.int32))
    return row_tok, pos.reshape(t, TOP_K), plan


def _final_body(x_ref, moe_ref, sh_ref, mod_ref, o_ref, *, gate_idx):
    o_ref[...] = x_ref[...] + mod_ref[gate_idx:gate_idx + 1, :] * (moe_ref[...] + sh_ref[...])


def gated_residual_sum(x, moe_y, shared_y, mod, l, lay, gate_idx, tm):
    t, d = x.shape
    seq = lambda i: lay.seq_of_row(i * tm)
    blk = pl.BlockSpec((tm, d), lambda i: (i, 0))
    return pl.pallas_call(
        functools.partial(_final_body, gate_idx=gate_idx),
        grid=(t // tm,),
        in_specs=[blk, blk, blk, pl.BlockSpec((None, None, N_MOD, d), lambda i: (l, seq(i), 0, 0))],
        out_specs=blk, out_shape=jax.ShapeDtypeStruct((t, d), f32),
        compiler_params=_cparams("parallel"),
        name="gated_residual_sum",
    )(x, moe_y, shared_y, mod)


def _col_offsets():
    sizes = [ATT_W, ATT_W, ATT_W, 3 * HY_W, ML_HEADS * ML_QK, ML_HEADS * ML_QK, ML_W, ML_W]
    return np.concatenate([[0], np.cumsum(sizes)]).tolist()


def kernel(x_prompt, x_sample, c_prompt, c_sample, ada_w, ada_b, norm1_g, norm2_g, w_in, w_out, qk_norm_g,
           diff_lambda, attn_out_g, hy_conv_w, hy_conv_b, hy_f_w1, hy_f_b1, hy_f_freq, hy_f_w2, hy_f_b2,
           hy_f_w3, hy_f_b3, hy_bias, hy_out_g, ml_gate_b, ml_out_g, router_w, router_b, exp_w_gate,
           exp_w_up, exp_w_down, sh_w_gate, sh_w_up, sh_w_down):
    nb1, s1, d = x_prompt.shape
    nb2, s2, _ = x_sample.shape
    lay = _Layout(nb1, s1, nb2, s2)
    t = lay.t
    offs = _col_offsets()
    n_main = offs[-1]
    q_off, k_off, v_off, hy_off, mq_off, mk_off, mv_off, mo_off = offs[:8]
    assert HY_W // HY_GROUPS == LANES and ATT_V == LANES

    x = jnp.concatenate([x_prompt.reshape(nb1 * s1, d), x_sample.reshape(nb2 * s2, d)], axis=0)
    c_all = jnp.concatenate([c_prompt, c_sample], axis=0).astype(f32)
    c_pad = jnp.pad(c_all, ((0, (-lay.nseq) % 8), (0, 0)))
    mod = ada_modulation(c_pad, ada_w, ada_b, tn=min(1024, d))
    mod = mod.reshape(DEPTH, c_pad.shape[0], N_MOD, d)

    w_in_b = w_in.astype(bf16)
    w_gate_b = jnp.pad(w_in[:, :, n_main:], ((0, 0), (0, 0), (0, LANES - (w_in.shape[2] - n_main)))).astype(bf16)
    w_out_b = w_out.astype(bf16)
    tf_moe, tn_moe = min(TF_MOE, exp_w_gate.shape[3]), min(TN_MOE, d)
    dft = {sl: dft_matrices(sl) for sl in sorted({s1, s2})}
    slopes, slope_lanes = alibi_slope_lanes()

    tm_mm = math.gcd(TM_MM, s2)
    for l in range(DEPTH):
        lam_init = 0.8 - 0.6 * math.exp(-0.3 * l)
        h1 = norm_modulate(x, mod, norm1_g, l, lay, 0, 1)[0]
        proj = matmul(h1, w_in_b, l, n_main, tm_mm, TN_IN)
        gates = matmul(h1, w_gate_b, l, LANES, tm_mm, LANES)

        lp = diff_lambda[l].astype(f32)
        lam = jnp.exp(jnp.sum(lp[0] * lp[1])) - jnp.exp(jnp.sum(lp[2] * lp[3])) + lam_init
        scal = jnp.concatenate([slopes, lam[None], jnp.zeros((3,), f32)])
        qn, kn, vb = attn_prepare(proj, qk_norm_g[l].reshape(2, ATT_V).astype(f32), lay, tm_mm)
        out_g = attn_out_g[l].reshape(1, ATT_V).astype(f32)
        ya = jnp.concatenate([
            attention_group(scal, slope_lanes, qn, kn, vb, out_g, rs, sl, ns, TQ_ATT, TK_ATT, TW_ATT, 1.0 - lam_init)
            for rs, sl, ns in lay.groups()], axis=0)

        u, ub = hyena_short_conv(proj, hy_off, hy_conv_w, hy_conv_b, l, lay, math.gcd(TM_CONV, s2),
                                 math.gcd(hy_off, HY_W, 512))
        yb_parts = []
        for rs, sl, ns in lay.groups():
            cos_m, nsin_m, alt8 = dft[sl]
            p, q = hyena_time_filters(sl, hy_f_w1[l], hy_f_b1[l], hy_f_freq[l], hy_f_w2[l], hy_f_b2[l],
                                      hy_f_w3[l], hy_f_b3[l])
            hr, hi = hyena_filter_spectrum(cos_m, nsin_m, alt8, p, q, min(TM_DFT, sl), TN_DFT)
            yb_parts.append(hyena_group(u, ub, (cos_m, nsin_m, alt8, hr, hi), hy_bias, hy_out_g, l, rs, sl, ns))
        yb = jnp.concatenate(yb_parts, axis=0)

        gcol, grow = mlstm_gate_tables(gates, ml_gate_b[l], ML_CHUNK)
        hf = mlstm_scan(proj, gcol, grow, lay, ML_CHUNK, False, mq_off, mk_off, mv_off)
        hb = mlstm_scan(proj, gcol, grow, lay, ML_CHUNK, True, mq_off, mk_off, mv_off)
        yc = mlstm_post(hf, hb, proj, mo_off, ml_out_g, l, tm_mm)

        ycat = jnp.concatenate([ya, yb, yc], axis=1)
        x = matmul_gated_residual(ycat, w_out_b, l, x, mod, lay, 2, tm_mm, TN_OUT)

        h2, h2f, logits = norm_modulate(x, mod, norm2_g, l, lay, 3, 4, router_w)
        routed = [route_tokens(logits[rs:rs + sl * ns], router_b[l]) for rs, sl, ns in lay.groups()]
        idx = jnp.concatenate([r[0] for r in routed], axis=0)
        gate = jnp.concatenate([r[1] for r in routed], axis=0)
        row_tok, pos, plan = dispatch_plan(idx, TB_MOE)
        x_rows = gather_rows(h2f, row_tok, plan[2], TB_MOE)
        tb_sh = math.gcd(TB_MOE, t)
        one = jnp.full((1,), t // tb_sh, jnp.int32)
        shared_y = grouped_ffn(h2, (one, jnp.zeros((1,), jnp.int32), one), sh_w_gate[:, None], sh_w_up[:, None],
                               sh_w_down[:, None], l, tb_sh, tf_moe, tn_moe, bf16)
        y_rows = grouped_ffn(x_rows, plan, exp_w_gate, exp_w_up, exp_w_down, l, TB_MOE, tf_moe, tn_moe, bf16)
        moe_y = jnp.sum(jnp.take(y_rows, pos, axis=0).astype(f32) * gate[:, :, None], axis=1)
        x = gated_residual_sum(x, moe_y, shared_y, mod, l, lay, 5, TM_ROW)

    return (x[:lay.p].reshape(nb1, s1, d), x[lay.p:].reshape(nb2, s2, d))
```

```python
import functools
import math

import jax
import jax.numpy as jnp
import numpy as np
from jax import lax
from jax.experimental import pallas as pl
from jax.experimental.pallas import tpu as pltpu

DEPTH = 2
ATT_QK = 64
ATT_V = 2 * ATT_QK
ATT_HEADS = 12
ATT_W = ATT_HEADS * ATT_V
HY_W = 1024
HY_ORDER = 2
HY_GROUPS = 8
HY_EMB = 33
HY_TARGET = 1e-2
HY_FAST = 0.3
HY_SLOW = 1.5
ML_V = 256
ML_QK = ML_V // 2
ML_HEADS = 6
ML_W = ML_HEADS * ML_V
N_EXPERTS = 64
TOP_K = 8
N_GROUPS = 8
TOPK_GROUPS = 4
ROUTED_SCALE = 2.5
N_MOD = 6
EPS = 1e-6

LANES = 128
VMEM_LIMIT = 56 * 1024 * 1024

ML_CHUNK = 256
TB_MOE = 512
TF_MOE = 512
TN_MOE = 2048
TM_MM = 1024
TN_IN = 768
TN_OUT = 512
TM_ROW = 256
TQ_ATT = 1024
TK_ATT = 512
TW_ATT = 2048
TM_DFT = 512
TN_DFT = 256
DFT_SHORT = 2048
TM_CONV = 1024

f32 = jnp.float32
bf16 = jnp.bfloat16
HI = lax.Precision.HIGHEST


def _cparams(*sem):
    return pltpu.CompilerParams(dimension_semantics=sem, vmem_limit_bytes=VMEM_LIMIT)


class _Layout:
    def __init__(self, nb1, s1, nb2, s2):
        self.nb1, self.s1, self.nb2, self.s2 = nb1, s1, nb2, s2
        self.p = nb1 * s1
        self.t = self.p + nb2 * s2
        self.nseq = nb1 + nb2

    def seq_of_row(self, r):
        return jnp.where(r < self.p, r // self.s1, self.nb1 + (r - self.p) // self.s2)

    def is_boundary(self, r):
        return jnp.where(r <= self.p, r % self.s1 == 0, (r - self.p) % self.s2 == 0)

    def groups(self):
        return ((0, self.s1, self.nb1), (self.p, self.s2, self.nb2))


def _ada_body(c_ref, w_ref, b_ref, o_ref):
    c = c_ref[...]
    s = c * jax.nn.sigmoid(c)
    o_ref[...] = jnp.dot(s, w_ref[...], preferred_element_type=f32, precision=HI) + b_ref[...]


def ada_modulation(c_pad, ada_w, ada_b, tn):
    r, d = c_pad.shape
    n = ada_w.shape[2]
    return pl.pallas_call(
        _ada_body,
        grid=(DEPTH, n // tn),
        in_specs=[pl.BlockSpec((r, d), lambda l, j: (0, 0)),
                  pl.BlockSpec((None, d, tn), lambda l, j: (l, 0, j)),
                  pl.BlockSpec((None, 1, tn), lambda l, j: (l, 0, j))],
        out_specs=pl.BlockSpec((None, r, tn), lambda l, j: (l, 0, j)),
        out_shape=jax.ShapeDtypeStruct((DEPTH, r, n), f32),
        compiler_params=_cparams("parallel", "parallel"),
        name="ada_modulation",
    )(c_pad, ada_w, ada_b.reshape(DEPTH, 1, n))


def _norm_mod_body(x_ref, mod_ref, g_ref, *rest, shift_idx, scale_idx, with_router):
    x = x_ref[...]
    ms = jnp.mean(x * x, axis=-1, keepdims=True)
    y = x * lax.rsqrt(ms + EPS) * g_ref[...]
    h = y * (1.0 + mod_ref[scale_idx:scale_idx + 1, :]) + mod_ref[shift_idx:shift_idx + 1, :]
    if with_router:
        rw_ref, h_ref, hf_ref, logit_ref = rest
        logit_ref[...] = jnp.dot(h, rw_ref[...], preferred_element_type=f32, precision=HI)
        hf_ref[...] = h
    else:
        (h_ref,) = rest
    h_ref[...] = h.astype(bf16)


def norm_modulate(x, mod, gain, l, lay, shift_idx, scale_idx, router_w=None):
    t, d = x.shape
    tm = TM_ROW
    seq = lambda i: lay.seq_of_row(i * tm)
    in_specs = [pl.BlockSpec((tm, d), lambda i: (i, 0)),
                pl.BlockSpec((None, None, N_MOD, d), lambda i: (l, seq(i), 0, 0)),
                pl.BlockSpec((None, 1, d), lambda i: (l, 0, 0))]
    args = [x, mod, gain.reshape(DEPTH, 1, d)]
    out_specs = [pl.BlockSpec((tm, d), lambda i: (i, 0))]
    out_shape = [jax.ShapeDtypeStruct((t, d), bf16)]
    if router_w is not None:
        e = router_w.shape[2]
        in_specs.append(pl.BlockSpec((None, d, e), lambda i: (l, 0, 0)))
        args.append(router_w)
        out_specs += [pl.BlockSpec((tm, d), lambda i: (i, 0)), pl.BlockSpec((tm, e), lambda i: (i, 0))]
        out_shape += [jax.ShapeDtypeStruct((t, d), f32), jax.ShapeDtypeStruct((t, e), f32)]
    return pl.pallas_call(
        functools.partial(_norm_mod_body, shift_idx=shift_idx, scale_idx=scale_idx,
                          with_router=router_w is not None),
        grid=(t // tm,), in_specs=in_specs, out_specs=out_specs, out_shape=out_shape,
        compiler_params=_cparams("parallel"),
        name="norm_modulate",
    )(*args)


def _mm_body(a_ref, b_ref, o_ref):
    o_ref[...] = jnp.dot(a_ref[...], b_ref[...], preferred_element_type=f32).astype(o_ref.dtype)


def matmul(a, b, l, n_out, tm, tn, out_dtype=f32):
    m, k = a.shape
    return pl.pallas_call(
        _mm_body,
        grid=(m // tm, n_out // tn),
        in_specs=[pl.BlockSpec((tm, k), lambda i, j: (i, 0)),
                  pl.BlockSpec((None, k, tn), lambda i, j: (l, 0, j))],
        out_specs=pl.BlockSpec((tm, tn), lambda i, j: (i, j)),
        out_shape=jax.ShapeDtypeStruct((m, n_out), out_dtype),
        compiler_params=_cparams("parallel", "parallel"),
        name="matmul",
    )(a, b)


def _mm_res_body(a_ref, b_ref, res_ref, mod_ref, o_ref, *, gate_idx):
    acc = jnp.dot(a_ref[...], b_ref[...], preferred_element_type=f32)
    o_ref[...] = res_ref[...] + mod_ref[gate_idx:gate_idx + 1, :] * acc


def matmul_gated_residual(a, b, l, res, mod, lay, gate_idx, tm, tn):
    m, k = a.shape
    n = res.shape[1]
    seq = lambda i: lay.seq_of_row(i * tm)
    return pl.pallas_call(
        functools.partial(_mm_res_body, gate_idx=gate_idx),
        grid=(m // tm, n // tn),
        in_specs=[pl.BlockSpec((tm, k), lambda i, j: (i, 0)),
                  pl.BlockSpec((None, k, tn), lambda i, j: (l, 0, j)),
                  pl.BlockSpec((tm, tn), lambda i, j: (i, j)),
                  pl.BlockSpec((None, None, N_MOD, tn), lambda i, j: (l, seq(i), 0, j))],
        out_specs=pl.BlockSpec((tm, tn), lambda i, j: (i, j)),
        out_shape=jax.ShapeDtypeStruct((m, n), f32),
        compiler_params=_cparams("parallel", "parallel"),
        name="matmul_gated_residual",
    )(a, b, res, mod)


POS_BLOCK = 1024
POS_LANE0 = ATT_QK
LOG2E = math.log2(math.e)


def _attn_pre_body(q_ref, k_ref, v_ref, g_ref, qn_ref, kn_ref, vb_ref, *, tm, lay):
    lane = lax.broadcasted_iota(jnp.int32, q_ref.shape, 1)
    lo = lane < ATT_QK

    def half_norm(x, g):
        sq = x * x
        s_lo = jnp.sum(jnp.where(lo, sq, 0.0), axis=-1, keepdims=True)
        s_hi = jnp.sum(jnp.where(lo, 0.0, sq), axis=-1, keepdims=True)
        inv = jnp.where(lo, lax.rsqrt(s_lo / ATT_QK + EPS), lax.rsqrt(s_hi / ATT_QK + EPS))
        return x * inv * g

    qn = half_norm(q_ref[...], g_ref[0:1, :]) * (ATT_QK ** -0.5 * LOG2E)
    kn = half_norm(k_ref[...], g_ref[1:2, :])
    row = pl.program_id(0) * tm + lax.broadcasted_iota(jnp.int32, q_ref.shape, 0)
    r = jnp.where(row < lay.p, row % lay.s1, (row - lay.p) % lay.s2) % POS_BLOCK
    hi = ((r // LANES) * LANES).astype(f32)
    lo_part = (r % LANES).astype(f32)
    pos = jnp.where((lane >= POS_LANE0) & (lane < POS_LANE0 + 3), hi,
                    jnp.where((lane >= POS_LANE0 + 3) & (lane < POS_LANE0 + 6), lo_part, 0.0))
    for mi in range(2):
        qm = qn if mi == 0 else pltpu.roll(qn, ATT_QK, axis=1)
        km = kn if mi == 0 else pltpu.roll(kn, ATT_QK, axis=1)
        qn_ref[mi] = jnp.where(lo, qm, 0.0).astype(bf16)
        kn_ref[mi] = jnp.where(lo, km, pos).astype(bf16)
    vb_ref[...] = v_ref[...].astype(bf16)


def attn_prepare(proj, qk_g2, lay, tm):
    t = proj.shape[0]
    hd = ATT_HEADS
    blk = lambda off: pl.BlockSpec((tm, ATT_V), lambda i, h: (i, off + h))
    qk_spec = pl.BlockSpec((None, 2, tm, ATT_V), lambda i, h: (h, 0, i, 0))
    qk_shp = jax.ShapeDtypeStruct((hd, 2, t, ATT_V), bf16)
    return pl.pallas_call(
        functools.partial(_attn_pre_body, tm=tm, lay=lay),
        grid=(t // tm, hd),
        in_specs=[blk(0), blk(hd), blk(2 * hd), pl.BlockSpec((2, ATT_V), lambda i, h: (0, 0))],
        out_specs=[qk_spec, qk_spec, pl.BlockSpec((None, tm, ATT_V), lambda i, h: (h, i, 0))],
        out_shape=[qk_shp, qk_shp, jax.ShapeDtypeStruct((hd, t, ATT_V), bf16)],
        compiler_params=_cparams("parallel", "parallel"),
        name="attn_prepare",
    )(proj, proj, proj, qk_g2)


def alibi_slope_lanes():
    slopes = 2.0 ** (-8.0 * np.arange(1, ATT_HEADS + 1) / ATT_HEADS) * LOG2E
    c = jnp.asarray(slopes, f32)
    c1 = c.astype(bf16)
    c2 = (c - c1.astype(f32)).astype(bf16)
    c3 = (c - c1.astype(f32) - c2.astype(f32)).astype(bf16)
    pieces = jnp.stack([c1, c2, c3, c1, c2, c3], axis=1)
    lanes = jnp.zeros((ATT_HEADS, 1, ATT_V), bf16).at[:, 0, POS_LANE0:POS_LANE0 + 6].set(pieces)
    return c, lanes


def _attn_body(scal_ref, q_ref, k_ref, v_ref, cl_ref, g_ref, o_ref, m_ref, l_ref, acc_ref, s_ref,
               *, tq, tk, tw, seq_len, out_scale):
    h, qi = pl.program_id(0), pl.program_id(1)
    c = scal_ref[h]
    qpos0 = (qi * tq) % seq_len
    ipos = (qpos0 + lax.broadcasted_iota(jnp.int32, (tq, 1), 0)).astype(f32)
    lane = lax.broadcasted_iota(jnp.int32, (tq, ATT_V), 1)
    pos_lane = (lane >= POS_LANE0) & (lane < POS_LANE0 + 6)
    nt = (((1,), (1,)), ((), ()))
    m_ref[...] = jnp.full(m_ref.shape, -jnp.inf, f32)
    l_ref[...] = jnp.zeros(l_ref.shape, f32)
    acc_ref[...] = jnp.zeros(acc_ref.shape, f32)
    cl = cl_ref[...]

    def wide_tile(wb, carry):
        w0 = pl.multiple_of(wb * tw, tw)
        for sub in range(tw // tk):
            j0 = w0 + sub * tk
            cols = slice(sub * tk, (sub + 1) * tk)
            is_left = j0 + tk <= qpos0
            is_right = j0 >= qpos0 + tq
            sgn = jnp.where(is_left, 1.0, jnp.where(is_right, -1.0, 0.0))
            off = (sgn * c) * (((j0 // POS_BLOCK) * POS_BLOCK).astype(f32) - ipos)
            for mi in range(2):
                qv = jnp.where(pos_lane, sgn.astype(cl.dtype) * cl, q_ref[mi])
                s_ref[mi, :, cols] = lax.dot_general(qv, k_ref[mi, pl.ds(j0, tk), :], nt,
                                                     preferred_element_type=f32) + off

            @pl.when(jnp.logical_not(is_left | is_right))
            def _():
                rc = lax.broadcasted_iota(jnp.int32, (tq, tk), 0) - lax.broadcasted_iota(jnp.int32, (tq, tk), 1)
                bias = c * jnp.abs(rc + (qpos0 - j0)).astype(f32)
                for mi in range(2):
                    s_ref[mi, :, cols] = s_ref[mi, :, cols] - bias

        vt = v_ref[pl.ds(w0, tw), :]
        for mi in range(2):
            s = s_ref[mi]
            m_prev = m_ref[mi]
            m_new = jnp.maximum(m_prev, jnp.max(s, axis=-1, keepdims=True))
            alpha = jnp.exp2(m_prev - m_new)
            p = jnp.exp2(s - m_new)
            l_ref[mi] = alpha * l_ref[mi] + jnp.sum(p, axis=-1, keepdims=True)
            acc_ref[mi] = alpha * acc_ref[mi] + jnp.dot(p.astype(bf16), vt, preferred_element_type=f32)
            m_ref[mi] = m_new
        return carry

    lax.fori_loop(0, seq_len // tw, wide_tile, 0)

    lam = scal_ref[ATT_HEADS]
    o = acc_ref[0] / l_ref[0] - lam * (acc_ref[1] / l_ref[1])
    ms = jnp.mean(o * o, axis=-1, keepdims=True)
    o_ref[...] = (o * lax.rsqrt(ms + EPS) * g_ref[...] * out_scale).astype(o_ref.dtype)


def attention_group(scal, c_lanes, qn, kn, vb, out_g, row_start, seq_len, nseq, tq, tk, tw, out_scale):
    tq, tk, tw = min(tq, seq_len), min(tk, seq_len), min(tw, seq_len)
    assert POS_BLOCK % tk == 0 and tw % tk == 0
    q0, s0 = row_start // tq, row_start // seq_len
    rows = nseq * seq_len
    return pl.pallas_call(
        functools.partial(_attn_body, tq=tq, tk=tk, tw=tw, seq_len=seq_len, out_scale=out_scale),
        grid=(ATT_HEADS, rows // tq),
        in_specs=[pl.BlockSpec(memory_space=pltpu.SMEM),
                  pl.BlockSpec((None, 2, tq, ATT_V), lambda h, qi: (h, 0, q0 + qi, 0)),
                  pl.BlockSpec((None, 2, seq_len, ATT_V), lambda h, qi: (h, 0, s0 + (qi * tq) // seq_len, 0)),
                  pl.BlockSpec((None, seq_len, ATT_V), lambda h, qi: (h, s0 + (qi * tq) // seq_len, 0)),
                  pl.BlockSpec((None, 1, ATT_V), lambda h, qi: (h, 0, 0)),
                  pl.BlockSpec((1, ATT_V), lambda h, qi: (0, 0))],
        out_specs=pl.BlockSpec((tq, ATT_V), lambda h, qi: (qi, h)),
        out_shape=jax.ShapeDtypeStruct((rows, ATT_W), bf16),
        scratch_shapes=[pltpu.VMEM((2, tq, 1), f32), pltpu.VMEM((2, tq, 1), f32),
                        pltpu.VMEM((2, tq, ATT_V), f32), pltpu.VMEM((2, tq, tw), f32)],
        compiler_params=_cparams("parallel", "parallel"),
        name="diff_attention",
    )(scal, qn, kn, vb, c_lanes, out_g)


def _mlstm_body(q_ref, k_ref, v_ref, gc_ref, gr_ref, h_ref, c_ref, n_ref, m_ref, *, lc, lay, backward):
    c = pl.program_id(0)
    nchunks = pl.num_programs(0)
    heads = ML_HEADS
    ceff = nchunks - 1 - c if backward else c
    r0 = ceff * lc
    reset = lay.is_boundary(r0 + lc) if backward else lay.is_boundary(r0)

    @pl.when(reset)
    def _():
        c_ref[...] = jnp.zeros(c_ref.shape, f32)
        n_ref[...] = jnp.zeros(n_ref.shape, f32)
        m_ref[...] = jnp.zeros(m_ref.shape, f32)

    row = lax.broadcasted_iota(jnp.int32, (lc, lc), 0)
    col = lax.broadcasted_iota(jnp.int32, (lc, lc), 1)
    mask = (col >= row) if backward else (col <= row)
    gcol = gc_ref[...]
    lane = lax.broadcasted_iota(jnp.int32, gcol.shape, 1)
    d = 1 if backward else 0

    def col_of(idx):
        return jnp.sum(jnp.where(lane == idx, gcol, 0.0), axis=-1, keepdims=True)

    for h in range(heads):
        ia = d * heads + h
        a_col, i_col = col_of(ia), col_of(2 * heads + ia)
        a_row = gr_ref[ia:ia + 1, :]
        i_row = gr_ref[2 * heads + ia:2 * heads + ia + 1, :]
        g11 = gr_ref[4 * heads + ia:4 * heads + ia + 1, 0:1]
        m11 = m_ref[h][:, 0:1]
        qf = q_ref[:, h * ML_QK:(h + 1) * ML_QK] * (ML_QK ** -0.5)
        kf = k_ref[:, h * ML_QK:(h + 1) * ML_QK]
        qb, kb = qf.astype(bf16), kf.astype(bf16)
        vb = v_ref[:, h * ML_V:(h + 1) * ML_V].astype(bf16)
        dmat = jnp.where(mask, a_col - a_row + i_row, -jnp.inf)
        m_inter = a_col + m11
        mj = jnp.maximum(jnp.max(dmat, axis=-1, keepdims=True), m_inter)
        p = jnp.exp(dmat - mj)
        s = lax.dot_general(qb, kb, (((1,), (1,)), ((), ())), preferred_element_type=f32)
        wts = p * s
        inter = jnp.exp(m_inter - mj)
        cmat = c_ref[h]
        num = (jnp.dot(wts.astype(bf16), vb, preferred_element_type=f32)
               + inter * jnp.dot(qb, cmat.astype(bf16), preferred_element_type=f32))
        nvec = n_ref[h]
        den = jnp.sum(wts, axis=-1, keepdims=True) + inter * jnp.sum(qf * nvec, axis=-1, keepdims=True)
        h_ref[:, h * ML_V:(h + 1) * ML_V] = num / jnp.maximum(jnp.abs(den), jnp.exp(-mj))
        ws = g11 - a_col + i_col
        m_new = jnp.maximum(g11 + m11, jnp.max(ws, axis=0, keepdims=True))
        dec = jnp.exp(g11 + m11 - m_new)
        ek = jnp.exp(ws - m_new) * kf
        c_ref[h] = dec * cmat + lax.dot_general(ek.astype(bf16), vb, (((0,), (0,)), ((), ())),
                                                preferred_element_type=f32)
        n_ref[h] = dec * nvec + jnp.sum(ek, axis=0, keepdims=True)
        m_ref[h] = jnp.broadcast_to(m_new, (1, LANES))


def mlstm_scan(proj, gcol, grow, lay, lc, backward, q_off, k_off, v_off):
    t = proj.shape[0]
    nchunks = t // lc
    hq, hv = ML_HEADS * ML_QK, ML_W
    cidx = (lambda c: nchunks - 1 - c) if backward else (lambda c: c)
    return pl.pallas_call(
        functools.partial(_mlstm_body, lc=lc, lay=lay, backward=backward),
        grid=(nchunks,),
        in_specs=[pl.BlockSpec((lc, hq), lambda c: (cidx(c), q_off // hq)),
                  pl.BlockSpec((lc, hq), lambda c: (cidx(c), k_off // hq)),
                  pl.BlockSpec((lc, hv), lambda c: (cidx(c), v_off // hv)),
                  pl.BlockSpec((lc, LANES), lambda c: (cidx(c), 0)),
                  pl.BlockSpec((grow.shape[0], lc), lambda c: (0, cidx(c)))],
        out_specs=pl.BlockSpec((lc, hv), lambda c: (cidx(c), 0)),
        out_shape=jax.ShapeDtypeStruct((t, hv), f32),
        scratch_shapes=[pltpu.VMEM((ML_HEADS, ML_QK, ML_V), f32), pltpu.VMEM((ML_HEADS, 1, ML_QK), f32),
                        pltpu.VMEM((ML_HEADS, 1, LANES), f32)],
        compiler_params=_cparams("arbitrary"),
        name="mlstm_bwd" if backward else "mlstm_fwd",
    )(proj, proj, proj, gcol, grow)


def _mlstm_post_body(hf_ref, hb_ref, o_ref, g_ref, y_ref):
    hsum = hf_ref[...] + hb_ref[...]
    ms = jnp.mean(hsum * hsum, axis=-1, keepdims=True)
    y_ref[...] = (jax.nn.sigmoid(o_ref[...]) * (hsum * lax.rsqrt(ms + EPS) * g_ref[...])).astype(y_ref.dtype)


def mlstm_post(hf, hb, proj, o_off, ml_g, l, tm):
    t = hf.shape[0]
    blk = lambda off: pl.BlockSpec((tm, ML_V), lambda i, h: (i, off + h))
    return pl.pallas_call(
        _mlstm_post_body,
        grid=(t // tm, ML_HEADS),
        in_specs=[blk(0), blk(0), blk(o_off // ML_V), pl.BlockSpec((None, 1, ML_V), lambda i, h: (l, 0, h))],
        out_specs=blk(0),
        out_shape=jax.ShapeDtypeStruct((t, ML_W), bf16),
        compiler_params=_cparams("parallel", "parallel"),
        name="mlstm_post",
    )(hf, hb, proj, ml_g.reshape(DEPTH, 1, ML_W))


def mlstm_gate_tables(gates, gate_b, lc):
    t = gates.shape[0]
    hd = ML_HEADS
    g = gates[:, :4 * hd].reshape(t, 4, hd) + gate_b.astype(f32)
    i_f, f_f, i_b, f_b = g[:, 0], g[:, 1], g[:, 2], g[:, 3]
    lf_f = jax.nn.log_sigmoid(f_f).reshape(t // lc, lc, hd)
    lf_b = jax.nn.log_sigmoid(f_b).reshape(t // lc, lc, hd)
    a_f = jnp.cumsum(lf_f, axis=1)
    a_b = jnp.flip(jnp.cumsum(jnp.flip(lf_b, axis=1), axis=1), axis=1)
    tot_f = jnp.broadcast_to(a_f[:, -1:, :], a_f.shape)
    tot_b = jnp.broadcast_to(a_b[:, :1, :], a_b.shape)
    flat = lambda a: a.reshape(t, hd)
    cols = jnp.concatenate([flat(a_f), flat(a_b), i_f, i_b, flat(tot_f), flat(tot_b)], axis=1)
    gcol = jnp.pad(cols, ((0, 0), (0, LANES - cols.shape[1])))
    grow = jnp.pad(cols.T, ((0, (-cols.shape[1]) % 8), (0, 0)))
    return gcol, grow


def _hyconv_body(c_ref, p_ref, n_ref, w_ref, b_ref, u_ref, ub_ref, *, tm, lay):
    r0 = pl.program_id(0) * tm
    x = c_ref[...]
    prev_row = jnp.where(lay.is_boundary(r0), 0.0, p_ref[7:8, :])
    next_row = jnp.where(lay.is_boundary(r0 + tm), 0.0, n_ref[0:1, :])
    rid = lax.broadcasted_iota(jnp.int32, x.shape, 0)
    xm = jnp.where(rid == 0, prev_row, pltpu.roll(x, 1, axis=0))
    xp = jnp.where(rid == tm - 1, next_row, pltpu.roll(x, tm - 1, axis=0))
    u = xm * w_ref[0:1, :] + x * w_ref[1:2, :] + xp * w_ref[2:3, :] + b_ref[...]
    u_ref[...] = u
    ub_ref[...] = u.astype(bf16)


def hyena_short_conv(proj, hy_off, conv_w, conv_b, l, lay, tm, tc):
    t = proj.shape[0]
    width = 3 * HY_W
    c0 = hy_off // tc
    nblk8 = t // 8
    return pl.pallas_call(
        functools.partial(_hyconv_body, tm=tm, lay=lay),
        grid=(t // tm, width // tc),
        in_specs=[pl.BlockSpec((tm, tc), lambda i, j: (i, c0 + j)),
                  pl.BlockSpec((8, tc), lambda i, j: (jnp.maximum(i * (tm // 8) - 1, 0), c0 + j)),
                  pl.BlockSpec((8, tc), lambda i, j: (jnp.minimum((i + 1) * (tm // 8), nblk8 - 1), c0 + j)),
                  pl.BlockSpec((None, 3, tc), lambda i, j: (l, 0, j)),
                  pl.BlockSpec((None, 1, tc), lambda i, j: (l, 0, j))],
        out_specs=[pl.BlockSpec((tm, tc), lambda i, j: (i, j))] * 2,
        out_shape=[jax.ShapeDtypeStruct((t, width), f32), jax.ShapeDtypeStruct((t, width), bf16)],
        compiler_params=_cparams("parallel", "parallel"),
        name="hyena_short_conv",
    )(proj, proj, proj, conv_w, conv_b.reshape(DEPTH, 1, width))


def hyena_time_filters(seq_len, w1, b1, freq, w2, b2, w3, b3):
    t = jnp.linspace(0.0, 1.0, seq_len, dtype=f32)[:, None]
    bands = (HY_EMB - 1) // 2
    fr = jnp.linspace(1e-4, bands - 1, bands, dtype=f32)
    ang = (2.0 * math.pi / seq_len) * jnp.arange(seq_len, dtype=f32)[:, None] * fr[None, :]
    z = jnp.concatenate([t, jnp.cos(ang), -jnp.sin(ang)], axis=-1)
    fq = freq.astype(f32)
    h = jnp.sin(fq[0] * (jnp.dot(z, w1, precision=HI) + b1))
    h = jnp.sin(fq[1] * (jnp.dot(h, w2, precision=HI) + b2))
    h = (jnp.dot(h, w3, precision=HI) + b3).reshape(seq_len, HY_ORDER, 2, HY_W)
    deltas = jnp.linspace(math.log(HY_TARGET) / HY_SLOW, math.log(HY_TARGET) / HY_FAST, HY_W, dtype=f32)
    h = h * jnp.exp(-t * jnp.abs(deltas))[:, None, None, :]
    hf, hb = h[:, :, 0], h[:, :, 1]
    first = (jnp.arange(seq_len) == 0)[:, None, None]
    hf = jnp.where(first, hf + hb, hf)
    hb = jnp.where(first, 0.0, hb)
    p = (hf + hb).reshape(seq_len, HY_ORDER * HY_W)
    q = (hf - hb).reshape(seq_len, HY_ORDER * HY_W)
    return p.astype(bf16), q.astype(bf16)


def dft_matrices(seq_len):
    k = jnp.arange(seq_len, dtype=jnp.int32)[:, None]

    def table(n):
        ang = ((k * n[None, :]) % (2 * seq_len)).astype(f32) * (math.pi / seq_len)
        return jnp.cos(ang), jnp.sin(ang)

    ca, sa = table(jnp.arange(seq_len // LANES, dtype=jnp.int32) * LANES)
    cb, sb = table(jnp.arange(LANES, dtype=jnp.int32))
    ca, sa, cb, sb = ca[:, :, None], sa[:, :, None], cb[:, None, :], sb[:, None, :]
    cos_m = (ca * cb - sa * sb).reshape(seq_len, seq_len).astype(bf16)
    nsin_m = (-(sa * cb + ca * sb)).reshape(seq_len, seq_len).astype(bf16)
    alt = jnp.where(jnp.arange(seq_len) % 2 == 0, 1.0, -1.0).astype(bf16)
    alt8 = jnp.zeros((8, seq_len), bf16).at[0].set(alt)
    return cos_m, nsin_m, alt8


def _hy_filter_body(fc_ref, fs_ref, alt_ref, p_ref, q_ref, hr_ref, hi_ref):
    p = p_ref[...]
    hr_ref[...] = jnp.dot(fc_ref[...], p, preferred_element_type=f32)
    hi = jnp.dot(fs_ref[...], q_ref[...], preferred_element_type=f32)
    nyq = jnp.dot(alt_ref[...], p, preferred_element_type=f32)[0:1, :]
    grow = pl.program_id(0) * hr_ref.shape[0] + lax.broadcasted_iota(jnp.int32, hi.shape, 0)
    hi_ref[...] = jnp.where(grow == 0, nyq, hi)


def hyena_filter_spectrum(cos_m, nsin_m, alt8, p, q, tm, tn):
    seq_len, n = p.shape
    a_spec = pl.BlockSpec((tm, seq_len), lambda i, j: (i, 0))
    b_spec = pl.BlockSpec((seq_len, tn), lambda i, j: (0, j))
    o_spec = pl.BlockSpec((tm, tn), lambda i, j: (i, j))
    shp = jax.ShapeDtypeStruct((seq_len, n), f32)
    return pl.pallas_call(
        _hy_filter_body,
        grid=(seq_len // tm, n // tn),
        in_specs=[a_spec, a_spec, pl.BlockSpec((8, seq_len), lambda i, j: (0, 0)), b_spec, b_spec],
        out_specs=[o_spec, o_spec], out_shape=[shp, shp],
        compiler_params=_cparams("parallel", "parallel"),
        name="hyena_filter_spectrum",
    )(cos_m, nsin_m, alt8, p, q)


def _hy_fwd_body(fc_ref, fs_ref, alt_ref, z_ref, hr_ref, hi_ref, yr_ref, yi_ref, *, n_fft):
    z = z_ref[...]
    vr = jnp.dot(fc_ref[...], z, preferred_element_type=f32)
    vi = jnp.dot(fs_ref[...], z, preferred_element_type=f32)
    nyq = jnp.dot(alt_ref[...], z, preferred_element_type=f32)[0:1, :]
    hr, hi = hr_ref[...], hi_ref[...]
    grow = pl.program_id(0) * vr.shape[0] + lax.broadcasted_iota(jnp.int32, vr.shape, 0)
    row0 = grow == 0
    vihi = jnp.where(row0, nyq, vi) * hi
    yr = vr * hr - jnp.where(row0, 0.0, vihi)
    yi = jnp.where(row0, vihi, vr * hi + vi * hr)
    scale = jnp.where(row0, 1.0 / n_fft, 2.0 / n_fft)
    yr_ref[...] = (yr * scale).astype(bf16)
    yi_ref[...] = (yi * scale).astype(bf16)


def hyena_fwd(cos_m, nsin_m, alt8, zb, z_col, hr, hi, order, row_start, seq_len, nseq, tm, tn):
    tm, tn = min(tm, seq_len), min(tn, HY_W)
    rows = nseq * seq_len
    s0, i0 = row_start // seq_len, seq_len // tm
    a_spec = pl.BlockSpec((tm, seq_len), lambda i, s, j: (i, 0))
    h_spec = pl.BlockSpec((tm, tn), lambda i, s, j: (i, order * (HY_W // tn) + j))
    o_spec = pl.BlockSpec((tm, tn), lambda i, s, j: (s * i0 + i, j))
    shp = jax.ShapeDtypeStruct((rows, HY_W), bf16)
    return pl.pallas_call(
        functools.partial(_hy_fwd_body, n_fft=2 * seq_len),
        grid=(seq_len // tm, nseq, HY_W // tn),
        in_specs=[a_spec, a_spec, pl.BlockSpec((8, seq_len), lambda i, s, j: (0, 0)),
                  pl.BlockSpec((seq_len, tn), lambda i, s, j: (s0 + s, z_col // tn + j)),
                  h_spec, h_spec],
        out_specs=[o_spec, o_spec], out_shape=[shp, shp],
        compiler_params=_cparams("parallel", "parallel", "parallel"),
        name="hyena_fwd_dft",
    )(cos_m, nsin_m, alt8, zb, hr, hi)


def _hy_inv_body(fc_ref, ft_ref, yr_ref, yi_ref, z_ref, x_ref, d_ref, *rest, last):
    y = (jnp.dot(fc_ref[...], yr_ref[...], preferred_element_type=f32)
         + jnp.dot(ft_ref[...], yi_ref[...], preferred_element_type=f32))
    n = pl.program_id(0) * y.shape[0] + lax.broadcasted_iota(jnp.int32, (y.shape[0], 1), 0)
    y = y + jnp.where(n % 2 == 0, 1.0, -1.0) * yi_ref[0:1, :].astype(f32)
    z = x_ref[...] * (y + d_ref[...] * z_ref[...])
    if last:
        g_ref, o_ref = rest
        gw = HY_W // HY_GROUPS
        for gi in range(z.shape[1] // gw):
            seg = z[:, gi * gw:(gi + 1) * gw]
            ms = jnp.mean(seg * seg, axis=-1, keepdims=True)
            o_ref[:, gi * gw:(gi + 1) * gw] = (seg * lax.rsqrt(ms + EPS)
                                               * g_ref[:, gi * gw:(gi + 1) * gw]).astype(o_ref.dtype)
    else:
        o_ref, ob_ref = rest
        o_ref[...] = z
        ob_ref[...] = z.astype(bf16)


def hyena_inv(cos_m, nsin_m, yr, yi, z_arr, z_row0, z_col, u, x_col, d_bias, gain, l, order,
              row_start, seq_len, nseq, tm, tn):
    tm, tn = min(tm, seq_len), min(tn, HY_W)
    last = order == HY_ORDER - 1
    rows = nseq * seq_len
    i0 = seq_len // tm
    a_spec = pl.BlockSpec((tm, seq_len), lambda i, s, j: (i, 0))
    b_spec = pl.BlockSpec((seq_len, tn), lambda i, s, j: (s, j))
    e_spec = lambda row0, col: pl.BlockSpec((tm, tn), lambda i, s, j: (row0 // tm + s * i0 + i, col // tn + j))
    o_spec = pl.BlockSpec((tm, tn), lambda i, s, j: (s * i0 + i, j))
    in_specs = [a_spec, a_spec, b_spec, b_spec, e_spec(z_row0, z_col), e_spec(row_start, x_col),
                pl.BlockSpec((None, 1, tn), lambda i, s, j: (l * HY_ORDER + order, 0, j))]
    args = [cos_m, nsin_m, yr, yi, z_arr, u, d_bias.reshape(DEPTH * HY_ORDER, 1, HY_W)]
    if last:
        in_specs.append(pl.BlockSpec((None, 1, tn), lambda i, s, j: (l, 0, j)))
        args.append(gain.reshape(DEPTH, 1, HY_W))
        out_specs = o_spec
        out_shape = jax.ShapeDtypeStruct((rows, HY_W), bf16)
    else:
        out_specs = [o_spec, o_spec]
        out_shape = [jax.ShapeDtypeStruct((rows, HY_W), f32), jax.ShapeDtypeStruct((rows, HY_W), bf16)]
    return pl.pallas_call(
        functools.partial(_hy_inv_body, last=last),
        grid=(seq_len // tm, nseq, HY_W // tn),
        in_specs=in_specs, out_specs=out_specs, out_shape=out_shape,
        compiler_params=_cparams("parallel", "parallel", "parallel"),
        name="hyena_inv_dft",
    )(*args)


def hyena_group(u, ub, spec, d_bias, gain, l, row_start, seq_len, nseq):
    cos_m, nsin_m, alt8, hr, hi = spec
    tm, tn = (TM_DFT, TN_DFT) if seq_len > DFT_SHORT else (2 * TM_DFT, 2 * TN_DFT)
    grp = (row_start, seq_len, nseq, tm, tn)
    yr, yi = hyena_fwd(cos_m, nsin_m, alt8, ub, 0, hr, hi, 0, *grp)
    z1, z1b = hyena_inv(cos_m, nsin_m, yr, yi, u, row_start, 0, u, HY_W, d_bias, gain, l, 0, *grp)
    yr, yi = hyena_fwd(cos_m, nsin_m, alt8, z1b, 0, hr, hi, 1, 0, seq_len, nseq, tm, tn)
    return hyena_inv(cos_m, nsin_m, yr, yi, z1, 0, 0, u, 2 * HY_W, d_bias, gain, l, 1, *grp)


def _item_changed(w, e_ref, c_ref):
    prev = jnp.maximum(w - 1, 0)
    return (w == 0) | (e_ref[w] != e_ref[prev]) | (c_ref[w] != c_ref[prev])


def _ffn_up_body(ib_ref, ic_ref, ie_ref, n_ref, x_ref, wg_ref, wu_ref, h_ref, wgb_ref, wub_ref):
    w = pl.program_id(0)

    @pl.when(w < n_ref[0])
    def _():
        @pl.when(_item_changed(w, ie_ref, ic_ref))
        def _():
            wgb_ref[...] = wg_ref[...].astype(bf16)
            wub_ref[...] = wu_ref[...].astype(bf16)

        x = x_ref[...]
        g = jnp.dot(x, wgb_ref[...], preferred_element_type=f32)
        u = jnp.dot(x, wub_ref[...], preferred_element_type=f32)
        h_ref[...] = (g * jax.nn.sigmoid(g) * u).astype(h_ref.dtype)


def _ffn_down_body(ib_ref, ic_ref, ie_ref, n_ref, h_ref, wd_ref, y_ref, wdb_ref):
    w = pl.program_id(0)

    @pl.when(w < n_ref[0])
    def _():
        @pl.when(_item_changed(w, ie_ref, ic_ref))
        def _():
            wdb_ref[...] = wd_ref[...].astype(bf16)

        y_ref[...] = jnp.dot(h_ref[...], wdb_ref[...], preferred_element_type=f32).astype(y_ref.dtype)


def ffn_items(plan, nb, ncol):
    blocks_per_e, first_block_e, n_used = plan
    n_items = n_used[0] * ncol
    w = jnp.minimum(jnp.arange(nb * ncol, dtype=jnp.int32), n_items - 1)
    istart = first_block_e * ncol
    e = jnp.sum((istart[None, :] <= w[:, None]).astype(jnp.int32), axis=1) - 1
    local = w - istart[e]
    cnt = jnp.maximum(blocks_per_e[e], 1)
    item_c = (local // cnt).astype(jnp.int32)
    item_b = (first_block_e[e] + local % cnt).astype(jnp.int32)
    return item_b, item_c, e, n_items[None].astype(jnp.int32)


def grouped_ffn(x_rows, plan, wg, wu, wd, l, tb, tf, tn, out_dtype):
    rows, d = x_rows.shape
    fdim = wg.shape[3]
    nb = rows // tb
    h = pl.pallas_call(
        _ffn_up_body,
        grid_spec=pltpu.PrefetchScalarGridSpec(
            num_scalar_prefetch=4, grid=(nb * (fdim // tf),),
            in_specs=[pl.BlockSpec((tb, d), lambda w, ib, ic, ie, n: (ib[w], 0)),
                      pl.BlockSpec((None, None, d, tf), lambda w, ib, ic, ie, n: (l, ie[w], 0, ic[w])),
                      pl.BlockSpec((None, None, d, tf), lambda w, ib, ic, ie, n: (l, ie[w], 0, ic[w]))],
            out_specs=pl.BlockSpec((tb, tf), lambda w, ib, ic, ie, n: (ib[w], ic[w])),
            scratch_shapes=[pltpu.VMEM((d, tf), bf16), pltpu.VMEM((d, tf), bf16)]),
        out_shape=jax.ShapeDtypeStruct((rows, fdim), bf16),
        compiler_params=_cparams("arbitrary"),
        name="ffn_up",
    )(*ffn_items(plan, nb, fdim // tf), x_rows, wg, wu)
    return pl.pallas_call(
        _ffn_down_body,
        grid_spec=pltpu.PrefetchScalarGridSpec(
            num_scalar_prefetch=4, grid=(nb * (d // tn),),
            in_specs=[pl.BlockSpec((tb, fdim), lambda w, ib, ic, ie, n: (ib[w], 0)),
                      pl.BlockSpec((None, None, fdim, tn), lambda w, ib, ic, ie, n: (l, ie[w], 0, ic[w]))],
            out_specs=pl.BlockSpec((tb, tn), lambda w, ib, ic, ie, n: (ib[w], ic[w])),
            scratch_shapes=[pltpu.VMEM((fdim, tn), bf16)]),
        out_shape=jax.ShapeDtypeStruct((rows, d), out_dtype),
        compiler_params=_cparams("arbitrary"),
        name="ffn_down",
    )(*ffn_items(plan, nb, d // tn), h, wd)


def route_tokens(logits, router_b):
    t = logits.shape[0]
    per = N_EXPERTS // N_GROUPS
    s = jax.nn.sigmoid(logits)
    sb = s + router_b.astype(f32)

    def pop_max(x):
        m = jnp.max(x, axis=-1, keepdims=True)
        iota = lax.broadcasted_iota(jnp.int32, x.shape, x.ndim - 1)
        first = jnp.min(jnp.where(x == m, iota, x.shape[-1]), axis=-1, keepdims=True)
        hit = iota == first
        return m, first, hit, jnp.where(hit, -jnp.inf, x)

    m1, _, _, rest = pop_max(sb.reshape(t, N_GROUPS, per))
    gscore = (m1 + jnp.max(rest, axis=-1, keepdims=True))[..., 0]
    gmask = jnp.zeros(gscore.shape, bool)
    for _ in range(TOPK_GROUPS):
        _, _, hit, gscore = pop_max(gscore)
        gmask = gmask | hit
    cur = jnp.where(jnp.repeat(gmask, per, axis=1), sb, -jnp.inf)
    idx, gate = [], []
    for _ in range(TOP_K):
        _, first, hit, cur = pop_max(cur)
        idx.append(first)
        gate.append(jnp.sum(jnp.where(hit, s, 0.0), axis=-1, keepdims=True))
    idx = jnp.concatenate(idx, axis=1)
    gate = jnp.concatenate(gate, axis=1)
    gate = gate / jnp.sum(gate, axis=-1, keepdims=True) * ROUTED_SCALE
    return idx, gate


def _row_gather_body(nused_ref, tok_ref, src_ref, out_ref, buf_ref, sem):
    tb = out_ref.shape[0]

    @pl.when(pl.program_id(0) < nused_ref[0])
    def _():
        def issue(i, carry):
            for prio in range(2):
                r = 2 * i + prio
                pltpu.make_async_copy(src_ref.at[pl.ds(tok_ref[0, r], 1)], buf_ref.at[pl.ds(r, 1)],
                                      sem).start(priority=prio)
            return carry

        lax.fori_loop(0, tb // 2, issue, 0, unroll=4)
        pltpu.make_async_copy(src_ref.at[pl.ds(0, tb)], buf_ref, sem).wait()
        out_ref[...] = buf_ref[...].astype(out_ref.dtype)


def gather_rows(src, row_tok, n_used, tb):
    rows = row_tok.shape[0]
    d = src.shape[1]
    nb = rows // tb
    return pl.pallas_call(
        _row_gather_body,
        grid_spec=pltpu.PrefetchScalarGridSpec(
            num_scalar_prefetch=1, grid=(nb,),
            in_specs=[pl.BlockSpec((None, 1, tb), lambda b, n: (b, 0, 0), memory_space=pltpu.SMEM),
                      pl.BlockSpec(memory_space=pl.ANY)],
            out_specs=pl.BlockSpec((tb, d), lambda b, n: (b, 0)),
            scratch_shapes=[pltpu.VMEM((tb, d), src.dtype), pltpu.SemaphoreType.DMA(())]),
        out_shape=jax.ShapeDtypeStruct((rows, d), bf16),
        compiler_params=_cparams("arbitrary"),
        name="gather_rows",
    )(n_used, row_tok.reshape(nb, 1, tb), src)


def dispatch_plan(idx, tb):
    t = idx.shape[0]
    n = t * TOP_K
    e_flat = idx.reshape(n).astype(jnp.int32)
    onehot = (e_flat[:, None] == jnp.arange(N_EXPERTS, dtype=jnp.int32)[None, :]).astype(jnp.int32)
    csum = jnp.cumsum(onehot, axis=0)
    counts = csum[-1]
    rank = jnp.take_along_axis(csum, e_flat[:, None], axis=1)[:, 0] - 1
    padded = (counts + tb - 1) // tb * tb
    pends = jnp.cumsum(padded)
    pstarts = pends - padded
    pos = pstarts[e_flat] + rank
    nb = (n + N_EXPERTS * (tb - 1) + tb - 1) // tb
    tok = jnp.arange(n, dtype=jnp.int32) // TOP_K
    row_tok = jnp.zeros((nb * tb,), jnp.int32).at[pos].set(tok, unique_indices=True)
    plan = ((padded // tb).astype(jnp.int32), (pstarts // tb).astype(jnp.int32),
            (pends[-1:] // tb).astype(jnp.int32))
    return row_tok, pos.reshape(t, TOP_K), plan


def _final_body(x_ref, moe_ref, sh_ref, mod_ref, o_ref, *, gate_idx):
    o_ref[...] = x_ref[...] + mod_ref[gate_idx:gate_idx + 1, :] * (moe_ref[...] + sh_ref[...])


def gated_residual_sum(x, moe_y, shared_y, mod, l, lay, gate_idx, tm):
    t, d = x.shape
    seq = lambda i: lay.seq_of_row(i * tm)
    blk = pl.BlockSpec((tm, d), lambda i: (i, 0))
    return pl.pallas_call(
        functools.partial(_final_body, gate_idx=gate_idx),
        grid=(t // tm,),
        in_specs=[blk, blk, blk, pl.BlockSpec((None, None, N_MOD, d), lambda i: (l, seq(i), 0, 0))],
        out_specs=blk, out_shape=jax.ShapeDtypeStruct((t, d), f32),
        compiler_params=_cparams("parallel"),
        name="gated_residual_sum",
    )(x, moe_y, shared_y, mod)


def _col_offsets():
    sizes = [ATT_W, ATT_W, ATT_W, 3 * HY_W, ML_HEADS * ML_QK, ML_HEADS * ML_QK, ML_W, ML_W]
    return np.concatenate([[0], np.cumsum(sizes)]).tolist()


def kernel(x_prompt, x_sample, c_prompt, c_sample, ada_w, ada_b, norm1_g, norm2_g, w_in, w_out, qk_norm_g,
           diff_lambda, attn_out_g, hy_conv_w, hy_conv_b, hy_f_w1, hy_f_b1, hy_f_freq, hy_f_w2, hy_f_b2,
           hy_f_w3, hy_f_b3, hy_bias, hy_out_g, ml_gate_b, ml_out_g, router_w, router_b, exp_w_gate,
           exp_w_up, exp_w_down, sh_w_gate, sh_w_up, sh_w_down):
    nb1, s1, d = x_prompt.shape
    nb2, s2, _ = x_sample.shape
    lay = _Layout(nb1, s1, nb2, s2)
    t = lay.t
    offs = _col_offsets()
    n_main = offs[-1]
    q_off, k_off, v_off, hy_off, mq_off, mk_off, mv_off, mo_off = offs[:8]
    assert HY_W // HY_GROUPS == LANES and ATT_V == LANES

    x = jnp.concatenate([x_prompt.reshape(nb1 * s1, d), x_sample.reshape(nb2 * s2, d)], axis=0)
    c_all = jnp.concatenate([c_prompt, c_sample], axis=0).astype(f32)
    c_pad = jnp.pad(c_all, ((0, (-lay.nseq) % 8), (0, 0)))
    mod = ada_modulation(c_pad, ada_w, ada_b, tn=min(1024, d))
    mod = mod.reshape(DEPTH, c_pad.shape[0], N_MOD, d)

    w_in_b = w_in.astype(bf16)
    w_gate_b = jnp.pad(w_in[:, :, n_main:], ((0, 0), (0, 0), (0, LANES - (w_in.shape[2] - n_main)))).astype(bf16)
    w_out_b = w_out.astype(bf16)
    tf_moe, tn_moe = min(TF_MOE, exp_w_gate.shape[3]), min(TN_MOE, d)
    dft = {sl: dft_matrices(sl) for sl in sorted({s1, s2})}
    slopes, slope_lanes = alibi_slope_lanes()

    tm_mm = math.gcd(TM_MM, s2)
    for l in range(DEPTH):
        lam_init = 0.8 - 0.6 * math.exp(-0.3 * l)
        h1 = norm_modulate(x, mod, norm1_g, l, lay, 0, 1)[0]
        proj = matmul(h1, w_in_b, l, n_main, tm_mm, TN_IN)
        gates = matmul(h1, w_gate_b, l, LANES, tm_mm, LANES)

        lp = diff_lambda[l].astype(f32)
        lam = jnp.exp(jnp.sum(lp[0] * lp[1])) - jnp.exp(jnp.sum(lp[2] * lp[3])) + lam_init
        scal = jnp.concatenate([slopes, lam[None], jnp.zeros((3,), f32)])
        qn, kn, vb = attn_prepare(proj, qk_norm_g[l].reshape(2, ATT_V).astype(f32), lay, tm_mm)
        out_g = attn_out_g[l].reshape(1, ATT_V).astype(f32)
        ya = jnp.concatenate([
            attention_group(scal, slope_lanes, qn, kn, vb, out_g, rs, sl, ns, TQ_ATT, TK_ATT, TW_ATT, 1.0 - lam_init)
            for rs, sl, ns in lay.groups()], axis=0)

        u, ub = hyena_short_conv(proj, hy_off, hy_conv_w, hy_conv_b, l, lay, math.gcd(TM_CONV, s2),
                                 math.gcd(hy_off, HY_W, 512))
        yb_parts = []
        for rs, sl, ns in lay.groups():
            cos_m, nsin_m, alt8 = dft[sl]
            p, q = hyena_time_filters(sl, hy_f_w1[l], hy_f_b1[l], hy_f_freq[l], hy_f_w2[l], hy_f_b2[l],
                                      hy_f_w3[l], hy_f_b3[l])
            hr, hi = hyena_filter_spectrum(cos_m, nsin_m, alt8, p, q, min(TM_DFT, sl), TN_DFT)
            yb_parts.append(hyena_group(u, ub, (cos_m, nsin_m, alt8, hr, hi), hy_bias, hy_out_g, l, rs, sl, ns))
        yb = jnp.concatenate(yb_parts, axis=0)

        gcol, grow = mlstm_gate_tables(gates, ml_gate_b[l], ML_CHUNK)
        hf = mlstm_scan(proj, gcol, grow, lay, ML_CHUNK, False, mq_off, mk_off, mv_off)
        hb = mlstm_scan(proj, gcol, grow, lay, ML_CHUNK, True, mq_off, mk_off, mv_off)
        yc = mlstm_post(hf, hb, proj, mo_off, ml_out_g, l, tm_mm)

        ycat = jnp.concatenate([ya, yb, yc], axis=1)
        x = matmul_gated_residual(ycat, w_out_b, l, x, mod, lay, 2, tm_mm, TN_OUT)

        h2, h2f, logits = norm_modulate(x, mod, norm2_g, l, lay, 3, 4, router_w)
        routed = [route_tokens(logits[rs:rs + sl * ns], router_b[l]) for rs, sl, ns in lay.groups()]
        idx = jnp.concatenate([r[0] for r in routed], axis=0)
        gate = jnp.concatenate([r[1] for r in routed], axis=0)
        row_tok, pos, plan = dispatch_plan(idx, TB_MOE)
        x_rows = gather_rows(h2f, row_tok, plan[2], TB_MOE)
        tb_sh = math.gcd(TB_MOE, t)
        one = jnp.full((1,), t // tb_sh, jnp.int32)
        shared_y = grouped_ffn(h2, (one, jnp.zeros((1,), jnp.int32), one), sh_w_gate[:, None], sh_w_up[:, None],
                               sh_w_down[:, None], l, tb_sh, tf_moe, tn_moe, bf16)
        y_rows = grouped_ffn(x_rows, plan, exp_w_gate, exp_w_up, exp_w_down, l, TB_MOE, tf_moe, tn_moe, bf16)
        moe_y = jnp.sum(jnp.take(y_rows, pos, axis=0).astype(f32) * gate[:, :, None], axis=1)
        x = gated_residual_sum(x, moe_y, shared_y, mod, l, lay, 5, TM_ROW)

    return (x[:lay.p].reshape(nb1, s1, d), x[lay.p:].reshape(nb2, s2, d))
```
